```python
import math, functools
import jax, jax.numpy as jnp
from jax import lax
import numpy as np

D_MODEL = 1024
BATCH = 8
SEQ = 2048
DEPTH = 4
DEC_BATCH = 128
DEC_SEQ = 8
PAST_LEN = 16384
PAGE_SIZE = 128

F32 = jnp.float32
N_MIXERS = 4
N_META = 16
CHUNK = 128
EXPAND = 2
D_INNER = EXPAND * D_MODEL
NORM_EPS = 1e-6
L_LRU = (DEPTH + 3) // 4
L_S5 = (DEPTH + 2) // 4
L_RWKV = (DEPTH + 1) // 4
L_RET = DEPTH // 4
LRU_CONV_W = 4
LRU_BLOCKS = 16
LRU_BLOCK = D_INNER // LRU_BLOCKS
LRU_C = 8.0
S5_GROUP = 16
S5_GROUPS = D_INNER // S5_GROUP
S5_STATE = 64
RWKV_HEAD = 64
RWKV_HEADS = D_INNER // RWKV_HEAD
RWKV_DECAY_LORA = 64
RWKV_A_LORA = 64
RWKV_LN_EPS = 64e-5
RET_HEADS = 4
RET_DK = D_MODEL // RET_HEADS
RET_DV = D_INNER // RET_HEADS
ROPE_BASE = 10000.0

kernel_name = 'hybrid_lru_s5_rwkv7_retention_step'


def rms_norm(x, g):
    xf = x.astype(F32)
    y = xf * lax.rsqrt(jnp.mean(xf * xf, axis=-1, keepdims=True) + NORM_EPS)
    return (y * g.astype(F32)).astype(x.dtype)


def rope(x, pos):
    half = x.shape[-1] // 2
    inv = ROPE_BASE ** (-jnp.arange(half, dtype=F32) / half)
    ang = pos.astype(F32)[:, None] * inv[None, :]
    cos = jnp.cos(ang)[None, :, None, :]
    sin = jnp.sin(ang)[None, :, None, :]
    x1, x2 = x[..., :half], x[..., half:]
    return jnp.concatenate([x1 * cos - x2 * sin, x2 * cos + x1 * sin], axis=-1)


def _lin_comb(l, r):
    return (l[0] * r[0], r[0] * l[1] + r[1])


def _cmul(ar, ai, br, bi):
    return (ar * br - ai * bi, ar * bi + ai * br)


def _clin_comb(l, r):
    a_r, a_i = _cmul(l[0], l[1], r[0], r[1])
    t_r, t_i = _cmul(r[0], r[1], l[2], l[3])
    return (a_r, a_i, t_r + r[2], t_i + r[3])


def lru_chunk(p, carry, xn, pos0):
    conv_buf, h0 = carry
    b, l, _ = xn.shape
    uz = xn @ p['w_in']
    u, gate = uz[..., :D_INNER], uz[..., D_INNER:]
    ext = jnp.concatenate([conv_buf.astype(u.dtype), u], axis=1)
    cw = p['conv_w']
    xc = p['conv_b'] + sum(ext[:, j:j + l] * cw[j] for j in range(LRU_CONV_W))
    xb = xc.reshape(b, l, LRU_BLOCKS, LRU_BLOCK)
    gate_r = jax.nn.sigmoid((jnp.einsum('blhi,hij->blhj', xb, p['wa']) + p['ba']).astype(F32)).reshape(b, l, D_INNER)
    gate_i = jax.nn.sigmoid((jnp.einsum('blhi,hij->blhj', xb, p['wx']) + p['bx']).astype(F32)).reshape(b, l, D_INNER)
    log_a = -LRU_C * gate_r * jax.nn.softplus(-p['lam'].astype(F32))
    a = jnp.exp(log_a)
    bx = jnp.sqrt(-jnp.expm1(2.0 * log_a)) * gate_i * xc.astype(F32)
    bx = bx.at[:, 0].add(a[:, 0] * h0.astype(F32))
    _, h = lax.associative_scan(_lin_comb, (a, bx), axis=1)
    y = h.astype(xn.dtype) * jax.nn.silu(gate)
    return (ext[:, l:].astype(conv_buf.dtype), h[:, -1].astype(h0.dtype)), y @ p['w_out']


def s5_chunk(p, carry, xn, pos0):
    s_re, s_im = carry
    b, l, _ = xn.shape
    uz = xn @ p['w_in']
    u, gate = uz[..., :D_INNER], uz[..., D_INNER:]
    uf = u.astype(F32)
    dt = jnp.exp(p['log_dt'].astype(F32))[:, None]
    a_re, a_im = p['a_re'].astype(F32), p['a_im'].astype(F32)
    mag = jnp.exp(dt * a_re)
    ang = dt * a_im
    ab_re, ab_im = mag * jnp.cos(ang), mag * jnp.sin(ang)
    den = a_re * a_re + a_im * a_im
    f_re = ((ab_re - 1.0) * a_re + ab_im * a_im) / den
    f_im = (ab_im * a_re - (ab_re - 1.0) * a_im) / den
    b_re, b_im = p['b_re'].astype(F32), p['b_im'].astype(F32)
    bb_re = f_re[..., None] * b_re - f_im[..., None] * b_im
    bb_im = f_re[..., None] * b_im + f_im[..., None] * b_re
    ug = uf.reshape(b, l, S5_GROUPS, S5_GROUP)
    bu_re = jnp.einsum('blgc,gnc->blgn', ug, bb_re)
    bu_im = jnp.einsum('blgc,gnc->blgn', ug, bb_im)
    sr, si = s_re.astype(F32), s_im.astype(F32)
    bu_re = bu_re.at[:, 0].add(ab_re * sr - ab_im * si)
    bu_im = bu_im.at[:, 0].add(ab_re * si + ab_im * sr)
    shp = bu_re.shape
    _, _, x_re, x_im = lax.associative_scan(
        _clin_comb, (jnp.broadcast_to(ab_re, shp), jnp.broadcast_to(ab_im, shp), bu_re, bu_im), axis=1)
    c_re, c_im = p['c_re'].astype(F32), p['c_im'].astype(F32)
    y = jnp.einsum('blgn,gcn->blgc', x_re, c_re) - jnp.einsum('blgn,gcn->blgc', x_im, c_im)
    y = y.reshape(b, l, D_INNER) + p['d'].astype(F32) * uf
    y = jax.nn.gelu(y).astype(xn.dtype)
    y = y * jax.nn.sigmoid(y @ p['glu_w'] + p['glu_b'])
    y = y * jax.nn.silu(gate)
    return (x_re[:, -1].astype(s_re.dtype), x_im[:, -1].astype(s_im.dtype)), y @ p['w_out']


def rwkv_chunk(p, carry, xn, pos0):
    x_prev, s0 = carry
    b, l, _ = xn.shape
    shifted = jnp.concatenate([x_prev[:, None].astype(xn.dtype), xn[:, :-1]], axis=1)
    xx = shifted - xn
    mu = p['mu']
    xr, xw, xk, xv, xa, xg = (xn + xx * mu[n] for n in range(6))
    r = xr @ p['w_r']
    k = xk @ p['w_k']
    v = xv @ p['w_v']
    g = jax.nn.silu(xg @ p['w_g'])
    w_raw = (p['w0'] + jnp.tanh(xw @ p['w1']) @ p['w2']).astype(F32)
    decay = jnp.exp(-jnp.exp(-jax.nn.softplus(-w_raw) - 0.5))
    a = jax.nn.sigmoid((p['a0'] + (xa @ p['a1']) @ p['a2']).astype(F32))
    hd = lambda t: t.astype(F32).reshape(b, l, RWKV_HEADS, RWKV_HEAD)
    per_head = lambda t: t.astype(F32).reshape(RWKV_HEADS, RWKV_HEAD)
    r, k, v, a, decay = hd(r), hd(k), hd(v), hd(a), hd(decay)
    kk = k * per_head(p['k_k'])
    kk = kk * lax.rsqrt(jnp.maximum(jnp.sum(kk * kk, axis=-1, keepdims=True), 1e-24))
    k = k * (1.0 + (a - 1.0) * per_head(p['k_a']))
    tm = lambda t: jnp.swapaxes(t, 0, 1)

    def step(s, inp):
        r_t, w_t, k_t, v_t, a_t, b_t = inp
        sa = jnp.einsum('bhvk,bhk->bhv', s, a_t)
        s = s * w_t[:, :, None, :] + sa[..., None] * b_t[:, :, None, :] + v_t[..., None] * k_t[:, :, None, :]
        return s, jnp.einsum('bhvk,bhk->bhv', s, r_t)

    s_new, y = lax.scan(step, s0.astype(F32), (tm(r), tm(decay), tm(k), tm(v), tm(-kk), tm(kk * a)))
    y = tm(y)
    mean = jnp.mean(y, axis=-1, keepdims=True)
    var = jnp.mean(jnp.square(y - mean), axis=-1, keepdims=True)
    yn = (y - mean) * lax.rsqrt(var + RWKV_LN_EPS) * per_head(p['ln_w']) + per_head(p['ln_b'])
    bonus = jnp.sum(r * k * per_head(p['r_k']), axis=-1, keepdims=True) * v
    out = (yn + bonus).reshape(b, l, D_INNER).astype(xn.dtype) * g
    return (xn[:, -1].astype(x_prev.dtype), s_new.astype(s0.dtype)), out @ p['w_o']


def ret_chunk(p, carry, xn, pos0):
    (s0,) = carry
    b, l, _ = xn.shape
    pos = pos0 + jnp.arange(l, dtype=jnp.int32)
    q = rope((xn @ p['w_q']).astype(F32).reshape(b, l, RET_HEADS, RET_DK), pos)
    k = rope((xn @ p['w_k']).astype(F32).reshape(b, l, RET_HEADS, RET_DK), pos) * (RET_DK ** -0.5)
    v = (xn @ p['w_v']).astype(F32).reshape(b, l, RET_HEADS, RET_DV)
    g = jax.nn.silu(xn @ p['w_g'])
    log_g = jnp.log1p(-jnp.exp2(-5.0 - jnp.arange(RET_HEADS, dtype=F32)))
    n = jnp.arange(l, dtype=F32)
    diff = n[:, None] - n[None, :]
    mask = jnp.where(diff[None] >= 0, jnp.exp(diff[None] * log_g[:, None, None]), 0.0)
    scores = jnp.einsum('blhd,bmhd->bhlm', q, k) * mask
    s0f = s0.astype(F32)
    y = jnp.einsum('bhlm,bmhe->blhe', scores, v)
    y = y + jnp.einsum('blhd,bhde->blhe', q, s0f) * jnp.exp((n[:, None] + 1.0) * log_g)[None, :, :, None]
    kw = k * jnp.exp((l - 1.0 - n)[:, None] * log_g)[None, :, :, None]
    s_new = jnp.exp(l * log_g)[None, :, None, None] * s0f + jnp.einsum('bmhd,bmhe->bhde', kw, v)
    y = y * lax.rsqrt(jnp.mean(y * y, axis=-1, keepdims=True) + NORM_EPS)
    out = y.reshape(b, l, D_INNER).astype(xn.dtype) * g
    return (s_new.astype(s0.dtype),), out @ p['w_o']


def run_prompt(chunk_fn, carry, xn):
    b, t, d = xn.shape
    carry, y_meta = chunk_fn(carry, xn[:, :N_META], 0)
    n_chunks = (t - N_META) // CHUNK
    xc = jnp.swapaxes(xn[:, N_META:].reshape(b, n_chunks, CHUNK, d), 0, 1)
    starts = N_META + CHUNK * jnp.arange(n_chunks, dtype=jnp.int32)

    def body(c, inp):
        start, xk = inp
        return chunk_fn(c, xk, start)

    carry, yc = lax.scan(body, carry, (starts, xc))
    y = jnp.swapaxes(yc, 0, 1).reshape(b, n_chunks * CHUNK, yc.shape[-1])
    return carry, jnp.concatenate([y_meta.astype(y.dtype), y], axis=1)


def prompt_init_carry(m, b):
    z = lambda *s: jnp.zeros(s, F32)
    if m == 0:
        return (z(b, LRU_CONV_W - 1, D_INNER), z(b, D_INNER))
    if m == 1:
        return (z(b, S5_GROUPS, S5_STATE), z(b, S5_GROUPS, S5_STATE))
    if m == 2:
        return (z(b, D_MODEL), z(b, RWKV_HEADS, RWKV_HEAD, RWKV_HEAD))
    return (z(b, RET_HEADS, RET_DK, RET_DV),)


def setup_inputs(seed: int = 0) -> dict:
    key = jax.random.key(seed)
    keys = iter(jax.random.split(key, 96))

    def nrm(shape, scale):
        return scale * jax.random.normal(next(keys), shape, F32)

    def uni(shape, lo, hi):
        return jax.random.uniform(next(keys), shape, F32, lo, hi)

    D, E = D_MODEL, D_INNER
    sd, se = D ** -0.5, E ** -0.5
    u_lam = uni((L_LRU, E), 0.9, 0.999)
    s_lam = u_lam ** (1.0 / LRU_C)
    w0_base = jnp.tile(jnp.linspace(-6.0, -1.0, RWKV_HEAD, dtype=F32), RWKV_HEADS)
    return {
        'x_prompt': nrm((BATCH, SEQ, D), 1.0),
        'x_sample': nrm((DEC_BATCH, DEC_SEQ, D), 1.0),
        'state_lru_conv': nrm((L_LRU, DEC_BATCH, LRU_CONV_W - 1, E), 1.0),
        'state_lru_h': nrm((L_LRU, DEC_BATCH, E), 0.5),
        'state_s5_re': nrm((L_S5, DEC_BATCH, S5_GROUPS, S5_STATE), 0.1),
        'state_s5_im': nrm((L_S5, DEC_BATCH, S5_GROUPS, S5_STATE), 0.1),
        'state_rwkv_shift': nrm((L_RWKV, DEC_BATCH, D), 1.0),
        'state_rwkv_wkv': nrm((L_RWKV, DEC_BATCH, RWKV_HEADS, RWKV_HEAD, RWKV_HEAD), 0.1),
        'state_ret': nrm((L_RET, DEC_BATCH, RET_HEADS, RET_DK, RET_DV), 0.1),
        'meta_tokens': nrm((N_META, D), 1.0),
        'norm_pre': 1.0 + nrm((DEPTH, D), 0.02),
        'norm_post': 1.0 + nrm((DEPTH, D), 0.02),
        'lru_w_in': nrm((L_LRU, D, 2 * E), sd),
        'lru_conv_w': nrm((L_LRU, LRU_CONV_W, E), 0.5),
        'lru_conv_b': nrm((L_LRU, E), 0.01),
        'lru_wa': nrm((L_LRU, LRU_BLOCKS, LRU_BLOCK, LRU_BLOCK), LRU_BLOCK ** -0.5),
        'lru_ba': nrm((L_LRU, LRU_BLOCKS, LRU_BLOCK), 0.01),
        'lru_wx': nrm((L_LRU, LRU_BLOCKS, LRU_BLOCK, LRU_BLOCK), LRU_BLOCK ** -0.5),
        'lru_bx': nrm((L_LRU, LRU_BLOCKS, LRU_BLOCK), 0.01),
        'lru_lam': jnp.log(s_lam) - jnp.log1p(-s_lam),
        'lru_w_out': nrm((L_LRU, E, D), se),
        's5_w_in': nrm((L_S5, D, 2 * E), sd),
        's5_log_dt': uni((L_S5, S5_GROUPS), math.log(1e-3), math.log(1e-1)),
        's5_a_re': -0.5 + nrm((L_S5, S5_GROUPS, S5_STATE), 0.01),
        's5_a_im': jnp.pi * jnp.arange(S5_STATE, dtype=F32) + nrm((L_S5, S5_GROUPS, S5_STATE), 0.01),
        's5_b_re': nrm((L_S5, S5_GROUPS, S5_STATE, S5_GROUP), (2 * S5_GROUP) ** -0.5),
        's5_b_im': nrm((L_S5, S5_GROUPS, S5_STATE, S5_GROUP), (2 * S5_GROUP) ** -0.5),
        's5_c_re': nrm((L_S5, S5_GROUPS, S5_GROUP, S5_STATE), (2 * S5_STATE) ** -0.5),
        's5_c_im': nrm((L_S5, S5_GROUPS, S5_GROUP, S5_STATE), (2 * S5_STATE) ** -0.5),
        's5_d': nrm((L_S5, E), 1.0),
        's5_glu_w': nrm((L_S5, E, E), se),
        's5_glu_b': nrm((L_S5, E), 0.01),
        's5_w_out': nrm((L_S5, E, D), se),
        'rwkv_mu': uni((L_RWKV, 6, D), 0.0, 1.0),
        'rwkv_w_r': nrm((L_RWKV, D, E), sd),
        'rwkv_w_k': nrm((L_RWKV, D, E), sd),
        'rwkv_w_v': nrm((L_RWKV, D, E), sd),
        'rwkv_w_g': nrm((L_RWKV, D, E), sd),
        'rwkv_w0': w0_base + nrm((L_RWKV, E), 0.1),
        'rwkv_w1': nrm((L_RWKV, D, RWKV_DECAY_LORA), sd),
        'rwkv_w2': nrm((L_RWKV, RWKV_DECAY_LORA, E), 0.1 * RWKV_DECAY_LORA ** -0.5),
        'rwkv_a0': nrm((L_RWKV, E), 0.1),
        'rwkv_a1': nrm((L_RWKV, D, RWKV_A_LORA), sd),
        'rwkv_a2': nrm((L_RWKV, RWKV_A_LORA, E), 0.1 * RWKV_A_LORA ** -0.5),
        'rwkv_k_k': 0.85 + nrm((L_RWKV, E), 0.02),
        'rwkv_k_a': 1.0 + nrm((L_RWKV, E), 0.02),
        'rwkv_r_k': nrm((L_RWKV, E), 0.1),
        'rwkv_ln_w': 1.0 + nrm((L_RWKV, E), 0.02),
        'rwkv_ln_b': nrm((L_RWKV, E), 0.01),
        'rwkv_w_o': nrm((L_RWKV, E, D), se),
        'ret_w_q': nrm((L_RET, D, RET_HEADS * RET_DK), sd),
        'ret_w_k': nrm((L_RET, D, RET_HEADS * RET_DK), sd),
        'ret_w_v': nrm((L_RET, D, E), sd),
        'ret_w_g': nrm((L_RET, D, E), sd),
        'ret_w_o': nrm((L_RET, E, D), se),
    }


def reference(x_prompt, x_sample, state_lru_conv, state_lru_h, state_s5_re, state_s5_im,
              state_rwkv_shift, state_rwkv_wkv, state_ret, meta_tokens, norm_pre, norm_post,
              lru_w_in, lru_conv_w, lru_conv_b, lru_wa, lru_ba, lru_wx, lru_bx, lru_lam, lru_w_out,
              s5_w_in, s5_log_dt, s5_a_re, s5_a_im, s5_b_re, s5_b_im, s5_c_re, s5_c_im, s5_d,
              s5_glu_w, s5_glu_b, s5_w_out,
              rwkv_mu, rwkv_w_r, rwkv_w_k, rwkv_w_v, rwkv_w_g, rwkv_w0, rwkv_w1, rwkv_w2,
              rwkv_a0, rwkv_a1, rwkv_a2, rwkv_k_k, rwkv_k_a, rwkv_r_k, rwkv_ln_w, rwkv_ln_b, rwkv_w_o,
              ret_w_q, ret_w_k, ret_w_v, ret_w_g, ret_w_o):
    params = (
        dict(w_in=lru_w_in, conv_w=lru_conv_w, conv_b=lru_conv_b, wa=lru_wa, ba=lru_ba,
             wx=lru_wx, bx=lru_bx, lam=lru_lam, w_out=lru_w_out),
        dict(w_in=s5_w_in, log_dt=s5_log_dt, a_re=s5_a_re, a_im=s5_a_im, b_re=s5_b_re, b_im=s5_b_im,
             c_re=s5_c_re, c_im=s5_c_im, d=s5_d, glu_w=s5_glu_w, glu_b=s5_glu_b, w_out=s5_w_out),
        dict(mu=rwkv_mu, w_r=rwkv_w_r, w_k=rwkv_w_k, w_v=rwkv_w_v, w_g=rwkv_w_g, w0=rwkv_w0,
             w1=rwkv_w1, w2=rwkv_w2, a0=rwkv_a0, a1=rwkv_a1, a2=rwkv_a2, k_k=rwkv_k_k,
             k_a=rwkv_k_a, r_k=rwkv_r_k, ln_w=rwkv_ln_w, ln_b=rwkv_ln_b, w_o=rwkv_w_o),
        dict(w_q=ret_w_q, w_k=ret_w_k, w_v=ret_w_v, w_g=ret_w_g, w_o=ret_w_o),
    )
    chunk_fns = (lru_chunk, s5_chunk, rwkv_chunk, ret_chunk)
    sample_states = ((state_lru_conv, state_lru_h), (state_s5_re, state_s5_im),
                     (state_rwkv_shift, state_rwkv_wkv), (state_ret,))
    b = x_prompt.shape[0]
    meta = jnp.broadcast_to(meta_tokens[None].astype(x_prompt.dtype), (b, N_META, D_MODEL))
    hp = jnp.concatenate([meta, x_prompt], axis=1)
    hs = x_sample
    new_p = [[] for _ in range(N_MIXERS)]
    new_s = [[] for _ in range(N_MIXERS)]
    for i in range(DEPTH):
        m, j = i % N_MIXERS, i // N_MIXERS
        p = {name: w[j] for name, w in params[m].items()}
        fn = functools.partial(chunk_fns[m], p)
        carry_p, z_p = run_prompt(fn, prompt_init_carry(m, b), rms_norm(hp, norm_pre[i]))
        carry_s, z_s = fn(tuple(s[j] for s in sample_states[m]), rms_norm(hs, norm_pre[i]), PAST_LEN)
        hp = hp + rms_norm(z_p, norm_post[i])
        hs = hs + rms_norm(z_s, norm_post[i])
        new_p[m].append(carry_p)
        new_s[m].append(carry_s)

    def stk(lst, idx):
        return jnp.stack([c[idx] for c in lst])

    y_prompt = hp[:, N_META:]
    y_sample = hs
    return (y_prompt, y_sample,
            stk(new_p[0], 0), stk(new_p[0], 1), stk(new_p[1], 0), stk(new_p[1], 1),
            stk(new_p[2], 0), stk(new_p[2], 1), stk(new_p[3], 0),
            stk(new_s[0], 0), stk(new_s[0], 1), stk(new_s[1], 0), stk(new_s[1], 1),
            stk(new_s[2], 0), stk(new_s[2], 1), stk(new_s[3], 0))
```

```python
import functools
import math

import jax
import jax.numpy as jnp
from jax import lax
from jax.experimental import pallas as pl
from jax.experimental.pallas import tpu as pltpu

F32 = jnp.float32
BF16 = jnp.bfloat16

D_MODEL = 1024
D_INNER = 2048
N_META = 16
NORM_EPS = 1e-6
LRU_CONV_W = 4
LRU_BLOCKS = 16
LRU_BLOCK = 128
LRU_C = 8.0
S5_GROUP = 16
S5_GROUPS = 128
S5_STATE = 64
S5_KB = 8
RWKV_HEAD = 64
RWKV_HEADS = 32
RWKV_LN_EPS = 64e-5
RET_HEADS = 4
RET_DK = 256
RET_DV = 512
ROPE_BASE = 10000.0
PAST_LEN = 16384
RET_CHUNK = 128

LANES = 128
VMEM_LIMIT = 56 * 1024 * 1024


def _params(*sem):
    return pltpu.CompilerParams(dimension_semantics=sem, vmem_limit_bytes=VMEM_LIMIT)


def _const(shape):
    zeros = (0,) * len(shape)
    return pl.BlockSpec(shape, lambda *_: zeros)


def _divisor_tile(n, pref, mult):
    best = mult
    t = mult
    while t <= min(n, pref):
        if n % t == 0:
            best = t
        t += mult
    assert n % best == 0
    return best


def _rms(x, g):
    return x * lax.rsqrt(jnp.mean(x * x, axis=-1, keepdims=True) + NORM_EPS) * g


def _mm(a, w):
    return jnp.dot(a.astype(BF16), w, preferred_element_type=F32)


def _silu(x):
    return x * jax.nn.sigmoid(x)


def _softplus(x):
    return jnp.maximum(x, 0.0) + jnp.log1p(jnp.exp(-jnp.abs(x)))


def _lru_kernel(x_ref, cs_ref, h0_ref, gpre_ref, gpost_ref, win_ref, cw_ref, cb_ref,
                wa_ref, ba_ref, wx_ref, bx_ref, lam_ref, wout_ref,
                xo_ref, cso_ref, ho_ref,
                uz_ref, a_ref, b_ref, tail_ref, h_ref, *, nb, tt):
    rows = nb * tt
    e = D_INNER

    @pl.when(pl.program_id(0) == 0)
    def _():
        tail_ref[...] = cs_ref[...]
        h_ref[...] = h0_ref[...]

    xn = _rms(x_ref[...], gpre_ref[...])
    uz_ref[...] = _mm(xn, win_ref[...])

    def block(j, carry):
        sl = pl.ds(pl.multiple_of(j * LRU_BLOCK, LRU_BLOCK), LRU_BLOCK)
        ext = jnp.concatenate([tail_ref[:, sl], uz_ref[:, sl]], axis=0)
        cw = cw_ref[:, sl]
        xc = cb_ref[:, sl]
        for jj in range(LRU_CONV_W):
            xc = xc + ext[jj * nb:jj * nb + rows] * cw[jj:jj + 1]
        tail_ref[:, sl] = ext[rows:rows + (LRU_CONV_W - 1) * nb]
        gate_r = jax.nn.sigmoid(_mm(xc, wa_ref[j]) + ba_ref[:, sl])
        gate_i = jax.nn.sigmoid(_mm(xc, wx_ref[j]) + bx_ref[:, sl])
        log_a = -LRU_C * gate_r * _softplus(-lam_ref[:, sl])
        a = jnp.exp(log_a)
        a_ref[:, sl] = a
        b_ref[:, sl] = jnp.sqrt(-jnp.tanh(log_a) * (a * a + 1.0)) * gate_i * xc
        return carry

    lax.fori_loop(0, LRU_BLOCKS, block, 0)

    def step(t, carry):
        r = pl.ds(pl.multiple_of(t * nb, nb), nb)
        h = a_ref[r, :] * h_ref[...] + b_ref[r, :]
        h_ref[...] = h
        b_ref[r, :] = h
        return carry

    lax.fori_loop(0, tt, step, 0)

    y = b_ref[...] * _silu(uz_ref[:, e:])
    out = _mm(y, wout_ref[...])
    xo_ref[...] = x_ref[...] + _rms(out, gpost_ref[...])
    cso_ref[...] = tail_ref[...]
    ho_ref[...] = h_ref[...]


def _lru_layer(x, conv_state, h0, gpre, gpost, w, *, nb, tt):
    r, d = x.shape
    e = D_INNER
    rows = nb * tt
    ctail = (LRU_CONV_W - 1) * nb
    row = lambda i: (i, 0)
    return pl.pallas_call(
        functools.partial(_lru_kernel, nb=nb, tt=tt),
        grid=(r // rows,),
        in_specs=[
            pl.BlockSpec((rows, d), row), _const((ctail, e)), _const((nb, e)),
            _const((1, d)), _const((1, d)), _const((d, 2 * e)),
            _const((LRU_CONV_W, e)), _const((1, e)),
            _const((LRU_BLOCKS, LRU_BLOCK, LRU_BLOCK)), _const((1, e)),
            _const((LRU_BLOCKS, LRU_BLOCK, LRU_BLOCK)), _const((1, e)),
            _const((1, e)), _const((e, d)),
        ],
        out_specs=[pl.BlockSpec((rows, d), row), _const((ctail, e)), _const((nb, e))],
        out_shape=[jax.ShapeDtypeStruct((r, d), F32),
                   jax.ShapeDtypeStruct((ctail, e), F32),
                   jax.ShapeDtypeStruct((nb, e), F32)],
        scratch_shapes=[pltpu.VMEM((rows, 2 * e), F32), pltpu.VMEM((rows, e), F32),
                        pltpu.VMEM((rows, e), F32), pltpu.VMEM((ctail, e), F32),
                        pltpu.VMEM((nb, e), F32)],
        compiler_params=_params("arbitrary"),
        name="lru_layer",
    )(x, conv_state, h0, gpre, gpost, w["w_in"], w["conv_w"], w["conv_b"],
      w["wa"], w["ba"], w["wx"], w["bx"], w["lam"], w["w_out"])


def _s5_disc_kernel(logdt_ref, are_ref, aim_ref, bre_ref, bim_ref,
                    abre_ref, abim_ref, bbre_ref, bbim_ref):
    dt = jnp.exp(logdt_ref[...])
    a_re = are_ref[...]
    a_im = aim_ref[...]
    mag = jnp.exp(dt * a_re)
    ang = dt * a_im
    ab_re = mag * jnp.cos(ang)
    ab_im = mag * jnp.sin(ang)
    den = a_re * a_re + a_im * a_im
    f_re = ((ab_re - 1.0) * a_re + ab_im * a_im) / den
    f_im = (ab_im * a_re - (ab_re - 1.0) * a_im) / den
    abre_ref[...] = ab_re
    abim_ref[...] = ab_im
    b_re = bre_ref[...]
    b_im = bim_ref[...]
    fr = f_re[:, None, :]
    fi = f_im[:, None, :]
    bbre_ref[...] = fr * b_re - fi * b_im
    bbim_ref[...] = fr * b_im + fi * b_re


def _s5_discretize(log_dt, a_re, a_im, b_re, b_im):
    g, n = a_re.shape
    c = b_re.shape[1]
    return pl.pallas_call(
        _s5_disc_kernel,
        out_shape=[jax.ShapeDtypeStruct((g, n), F32), jax.ShapeDtypeStruct((g, n), F32),
                   jax.ShapeDtypeStruct((g, c, n), F32), jax.ShapeDtypeStruct((g, c, n), F32)],
        name="s5_discretize",
    )(log_dt.reshape(g, 1), a_re, a_im, b_re, b_im)


def _proj_kernel(x_ref, g_ref, w_ref, o_ref):
    o_ref[...] = _mm(_rms(x_ref[...], g_ref[...]), w_ref[...])


def _norm_proj(x, g, w, *, rows):
    r, d = x.shape
    n = w.shape[1]
    row = lambda i: (i, 0)
    return pl.pallas_call(
        _proj_kernel,
        grid=(r // rows,),
        in_specs=[pl.BlockSpec((rows, d), row), _const((1, d)), _const((d, n))],
        out_specs=pl.BlockSpec((rows, n), row),
        out_shape=jax.ShapeDtypeStruct((r, n), F32),
        compiler_params=_params("parallel"),
        name="norm_proj",
    )(x, g, w)


def _gelu_tanh(x):
    return 0.5 * x * (1.0 + jnp.tanh(math.sqrt(2.0 / math.pi) * (x + 0.044715 * (x * x * x))))


def _s5_core_kernel(u_ref, sre_ref, sim_ref, bw_ref, cw_ref, abre_ref, abim_ref, d_ref,
                    y_ref, sreo_ref, simo_ref,
                    bur_ref, bui_ref, xr_ref, xi_ref, *, nb, tt):
    gn = S5_GROUPS * S5_STATE // S5_KB
    ch = D_INNER // S5_KB

    @pl.when(pl.program_id(0) == 0)
    def _():
        xr_ref[...] = sre_ref[...]
        xi_ref[...] = sim_ref[...]

    for kb in range(S5_KB):
        bu = _mm(u_ref[:, kb * ch:(kb + 1) * ch], bw_ref[kb])
        bur_ref[:, kb * gn:(kb + 1) * gn] = bu[:, :gn]
        bui_ref[:, kb * gn:(kb + 1) * gn] = bu[:, gn:]

    def step(t, carry):
        r = pl.ds(pl.multiple_of(t * nb, nb), nb)
        ar = abre_ref[...]
        ai = abim_ref[...]
        xr = xr_ref[...]
        xi = xi_ref[...]
        nr = ar * xr - ai * xi + bur_ref[r, :]
        ni = ar * xi + ai * xr + bui_ref[r, :]
        xr_ref[...] = nr
        xi_ref[...] = ni
        bur_ref[r, :] = nr
        bui_ref[r, :] = ni
        return carry

    lax.fori_loop(0, tt, step, 0)

    for kb in range(S5_KB):
        y = (_mm(bur_ref[:, kb * gn:(kb + 1) * gn], cw_ref[kb, :gn, :])
             + _mm(bui_ref[:, kb * gn:(kb + 1) * gn], cw_ref[kb, gn:, :]))
        y = y + d_ref[:, kb * ch:(kb + 1) * ch] * u_ref[:, kb * ch:(kb + 1) * ch]
        y_ref[:, kb * ch:(kb + 1) * ch] = _gelu_tanh(y)

    sreo_ref[...] = xr_ref[...]
    simo_ref[...] = xi_ref[...]


def _s5_core(uz, s_re, s_im, bw, cw, ab_re, ab_im, dvec, *, nb, tt):
    r = uz.shape[0]
    e = D_INNER
    gn = S5_GROUPS * S5_STATE
    rows = nb * tt
    row = lambda i: (i, 0)
    return pl.pallas_call(
        functools.partial(_s5_core_kernel, nb=nb, tt=tt),
        grid=(r // rows,),
        in_specs=[pl.BlockSpec((rows, e), row), _const((nb, gn)), _const((nb, gn)),
                  _const(bw.shape), _const(cw.shape), _const((1, gn)), _const((1, gn)),
                  _const((1, e))],
        out_specs=[pl.BlockSpec((rows, e), row), _const((nb, gn)), _const((nb, gn))],
        out_shape=[jax.ShapeDtypeStruct((r, e), F32), jax.ShapeDtypeStruct((nb, gn), F32),
                   jax.ShapeDtypeStruct((nb, gn), F32)],
        scratch_shapes=[pltpu.VMEM((rows, gn), F32), pltpu.VMEM((rows, gn), F32),
                        pltpu.VMEM((nb, gn), F32), pltpu.VMEM((nb, gn), F32)],
        compiler_params=_params("arbitrary"),
        name="s5_core",
    )(uz, s_re, s_im, bw, cw, ab_re, ab_im, dvec)


def _s5_out_kernel(y_ref, gate_ref, x_ref, gluw_ref, glub_ref, wout_ref, gpost_ref, xo_ref):
    y = y_ref[...]
    y = y * jax.nn.sigmoid(_mm(y, gluw_ref[...]) + glub_ref[...])
    y = y * _silu(gate_ref[...])
    xo_ref[...] = x_ref[...] + _rms(_mm(y, wout_ref[...]), gpost_ref[...])


def _s5_out(y, uz, x, glu_w, glu_b, w_out, gpost, *, rows):
    r, d = x.shape
    e = D_INNER
    row = lambda i: (i, 0)
    return pl.pallas_call(
        _s5_out_kernel,
        grid=(r // rows,),
        in_specs=[pl.BlockSpec((rows, e), row), pl.BlockSpec((rows, e), lambda i: (i, 1)),
                  pl.BlockSpec((rows, d), row), _const((e, e)), _const((1, e)),
                  _const((e, d)), _const((1, d))],
        out_specs=pl.BlockSpec((rows, d), row),
        out_shape=jax.ShapeDtypeStruct((r, d), F32),
        compiler_params=_params("parallel"),
        name="s5_out",
    )(y, uz, x, glu_w, glu_b, w_out, gpost)


def _rwkv_proj_kernel(x_ref, xprev_ref, gpre_ref, mu_ref, wr_ref, wk_ref, wv_ref, wg_ref,
                      w0_ref, w1_ref, w2_ref, a0_ref, a1_ref, a2_ref,
                      r_ref, k_ref, v_ref, g_ref, dec_ref, a_ref, shift_ref,
                      prev_ref, *, nb, tt):
    rows = nb * tt

    @pl.when(pl.program_id(0) == 0)
    def _():
        prev_ref[...] = xprev_ref[...]

    xn = _rms(x_ref[...], gpre_ref[...])
    if tt > 1:
        shifted = jnp.concatenate([prev_ref[...], xn[:rows - nb]], axis=0)
    else:
        shifted = prev_ref[...]
    prev_ref[...] = xn[rows - nb:]
    shift_ref[...] = xn[rows - nb:]
    xx = shifted - xn
    mix = lambda n: xn + xx * mu_ref[n:n + 1, :]
    r_ref[...] = _mm(mix(0), wr_ref[...])
    k_ref[...] = _mm(mix(2), wk_ref[...])
    v_ref[...] = _mm(mix(3), wv_ref[...])
    g_ref[...] = _silu(_mm(mix(5), wg_ref[...]))
    w_raw = w0_ref[...] + _mm(jnp.tanh(_mm(mix(1), w1_ref[...])), w2_ref[...])
    dec_ref[...] = jnp.exp(-(jax.nn.sigmoid(w_raw) * math.exp(-0.5)))
    a_ref[...] = jax.nn.sigmoid(a0_ref[...] + _mm(_mm(mix(4), a1_ref[...]), a2_ref[...]))


def _rwkv_proj(x, x_prev, gpre, w, *, nb, tt):
    r, d = x.shape
    e = D_INNER
    rows = nb * tt
    lora = w["w1"].shape[1]
    row = lambda i: (i, 0)
    big = pl.BlockSpec((rows, e), row)
    return pl.pallas_call(
        functools.partial(_rwkv_proj_kernel, nb=nb, tt=tt),
        grid=(r // rows,),
        in_specs=[pl.BlockSpec((rows, d), row), _const((nb, d)), _const((1, d)), _const((6, d)),
                  _const((d, e)), _const((d, e)), _const((d, e)), _const((d, e)),
                  _const((1, e)), _const((d, lora)), _const((lora, e)),
                  _const((1, e)), _const((d, lora)), _const((lora, e))],
        out_specs=[big, big, big, big, big, big, _const((nb, d))],
        out_shape=[jax.ShapeDtypeStruct((r, e), F32)] * 6 + [jax.ShapeDtypeStruct((nb, d), F32)],
        scratch_shapes=[pltpu.VMEM((nb, d), F32)],
        compiler_params=_params("arbitrary"),
        name="rwkv_proj",
    )(x, x_prev, gpre, w["mu"], w["w_r"], w["w_k"], w["w_v"], w["w_g"],
      w["w0"], w["w1"], w["w2"], w["a0"], w["a1"], w["a2"])


def _rwkv_rec_kernel(r_ref, w_ref, k_ref, v_ref, a_ref, kk_ref, ka_ref, rk_ref, lnw_ref, lnb_ref,
                     s0_ref, o_ref, so_ref, s_ref, vec_ref, *, tt):
    n = RWKV_HEAD

    @pl.when(pl.program_id(1) == 0)
    def _():
        s_ref[...] = s0_ref[...]

    def token(t, carry):
        r = r_ref[t]
        k = k_ref[t]
        v = v_ref[t]
        a = a_ref[t]
        kk = k * kk_ref[...]
        kk = kk * lax.rsqrt(jnp.maximum(jnp.sum(kk * kk, axis=0, keepdims=True), 1e-24))
        k2 = k * (1.0 + (a - 1.0) * ka_ref[...])
        vec_ref[0] = -kk
        vec_ref[1] = kk * a
        vec_ref[2] = k2
        sa = jnp.zeros((n, LANES), F32)
        for q in range(n):
            sa = sa + s_ref[q] * vec_ref[0, q:q + 1, :]
        y = jnp.zeros((n, LANES), F32)
        for q in range(n):
            s_new = (s_ref[q] * w_ref[t, q:q + 1, :] + sa * vec_ref[1, q:q + 1, :]
                     + v * vec_ref[2, q:q + 1, :])
            s_ref[q] = s_new
            y = y + s_new * r_ref[t, q:q + 1, :]
        mean = jnp.mean(y, axis=0, keepdims=True)
        yc = y - mean
        var = jnp.mean(yc * yc, axis=0, keepdims=True)
        yn = yc * lax.rsqrt(var + RWKV_LN_EPS) * lnw_ref[...] + lnb_ref[...]
        bonus = jnp.sum(r * k2 * rk_ref[...], axis=0, keepdims=True) * v
        o_ref[t] = yn + bonus
        return carry

    lax.fori_loop(0, tt, token, 0)
    so_ref[...] = s_ref[...]


def _rwkv_rec(r, w, k, v, a, pk, s0, *, tt):
    t_len, n, p = r.shape
    tok = pl.BlockSpec((tt, n, LANES), lambda pi, ti: (ti, 0, pi))
    par = pl.BlockSpec((n, LANES), lambda pi, ti: (0, pi))
    st = pl.BlockSpec((n, n, LANES), lambda pi, ti: (0, 0, pi))
    return pl.pallas_call(
        functools.partial(_rwkv_rec_kernel, tt=tt),
        grid=(p // LANES, t_len // tt),
        in_specs=[tok] * 5 + [par] * 5 + [st],
        out_specs=[tok, st],
        out_shape=[jax.ShapeDtypeStruct((t_len, n, p), F32), jax.ShapeDtypeStruct((n, n, p), F32)],
        scratch_shapes=[pltpu.VMEM((n, n, LANES), F32), pltpu.VMEM((3, n, LANES), F32)],
        compiler_params=_params("parallel", "arbitrary"),
        name="rwkv_recurrence",
    )(r, w, k, v, a, pk["k_k"], pk["k_a"], pk["r_k"], pk["ln_w"], pk["ln_b"], s0)


def _gated_out_kernel(y_ref, g_ref, x_ref, w_ref, gpost_ref, xo_ref):
    y = y_ref[...] * g_ref[...]
    xo_ref[...] = x_ref[...] + _rms(_mm(y, w_ref[...]), gpost_ref[...])


def _gated_out(y, g, x, w, gpost, *, rows):
    r, d = x.shape
    e = D_INNER
    row = lambda i: (i, 0)
    return pl.pallas_call(
        _gated_out_kernel,
        grid=(r // rows,),
        in_specs=[pl.BlockSpec((rows, e), row), pl.BlockSpec((rows, e), row),
                  pl.BlockSpec((rows, d), row), _const((e, d)), _const((1, d))],
        out_specs=pl.BlockSpec((rows, d), row),
        out_shape=jax.ShapeDtypeStruct((r, d), F32),
        compiler_params=_params("parallel"),
        name="gated_out",
    )(y, g, x, w, gpost)


def _ret_proj_kernel(x_ref, gpre_ref, wq_ref, wk_ref, wv_ref, wg_ref, q_ref, k_ref, v_ref, g_ref):
    xn = _rms(x_ref[...], gpre_ref[...]).astype(BF16)
    q_ref[...] = jnp.dot(xn, wq_ref[...], preferred_element_type=F32)
    k_ref[...] = jnp.dot(xn, wk_ref[...], preferred_element_type=F32)
    v_ref[...] = jnp.dot(xn, wv_ref[...], preferred_element_type=F32)
    g_ref[...] = _silu(jnp.dot(xn, wg_ref[...], preferred_element_type=F32))


def _ret_proj(x, gpre, w, *, rows):
    r, d = x.shape
    e = D_INNER
    row = lambda i: (i, 0)
    return pl.pallas_call(
        _ret_proj_kernel,
        grid=(r // rows,),
        in_specs=[pl.BlockSpec((rows, d), row), _const((1, d)), _const((d, d)), _const((d, d)),
                  _const((d, e)), _const((d, e))],
        out_specs=[pl.BlockSpec((rows, d), row), pl.BlockSpec((rows, d), row),
                   pl.BlockSpec((rows, e), row), pl.BlockSpec((rows, e), row)],
        out_shape=[jax.ShapeDtypeStruct((r, d), F32), jax.ShapeDtypeStruct((r, d), F32),
                   jax.ShapeDtypeStruct((r, e), F32), jax.ShapeDtypeStruct((r, e), F32)],
        compiler_params=_params("parallel"),
        name="ret_proj",
    )(x, gpre, w["w_q"], w["w_k"], w["w_v"], w["w_g"])


def _ret_core_kernel(q_ref, kt_ref, v_ref, cos_ref, sin_ref, cost_ref, sint_ref, mask_ref,
                     qdec_ref, kdec_ref, gl_ref, s0_ref, y_ref, so_ref, s_ref):
    half = RET_DK // 2

    @pl.when(pl.program_id(1) == 0)
    def _():
        s_ref[...] = s0_ref[0]

    cos = cos_ref[...]
    sin = sin_ref[...]
    cost = cost_ref[...]
    sint = sint_ref[...]
    for h in range(RET_HEADS):
        q1 = q_ref[:, h * RET_DK:h * RET_DK + half]
        q2 = q_ref[:, h * RET_DK + half:(h + 1) * RET_DK]
        qr = jnp.concatenate([q1 * cos - q2 * sin, q2 * cos + q1 * sin], axis=1)
        k1 = kt_ref[0, h * RET_DK:h * RET_DK + half, :]
        k2 = kt_ref[0, h * RET_DK + half:(h + 1) * RET_DK, :]
        krt = jnp.concatenate([k1 * cost - k2 * sint, k2 * cost + k1 * sint], axis=0)
        krt = krt * (RET_DK ** -0.5)
        vh = v_ref[:, h * RET_DV:(h + 1) * RET_DV].astype(BF16)
        qb = qr.astype(BF16)
        scores = jnp.dot(qb, krt.astype(BF16), preferred_element_type=F32) * mask_ref[h]
        s_old = s_ref[h]
        y = jnp.dot(scores.astype(BF16), vh, preferred_element_type=F32)
        y = y + jnp.dot(qb, s_old.astype(BF16), preferred_element_type=F32) * qdec_ref[h]
        kw = (krt * kdec_ref[h]).astype(BF16)
        s_ref[h] = gl_ref[h] * s_old + jnp.dot(kw, vh, preferred_element_type=F32)
        y = y * lax.rsqrt(jnp.mean(y * y, axis=-1, keepdims=True) + NORM_EPS)
        y_ref[:, h * RET_DV:(h + 1) * RET_DV] = y
    so_ref[0] = s_ref[...]


def _ret_core(q, kt, v, tabs, s0, *, lc):
    b, dkh, t_len = kt.shape
    nc = t_len // lc
    e = D_INNER
    rowc = lambda bi, ci: (bi * nc + ci, 0)
    state = pl.BlockSpec((1, RET_HEADS, RET_DK, RET_DV), lambda bi, ci: (bi, 0, 0, 0))
    half = RET_DK // 2
    return pl.pallas_call(
        _ret_core_kernel,
        grid=(b, nc),
        in_specs=[pl.BlockSpec((lc, dkh), rowc),
                  pl.BlockSpec((1, dkh, lc), lambda bi, ci: (bi, 0, ci)),
                  pl.BlockSpec((lc, e), rowc),
                  pl.BlockSpec((lc, half), lambda bi, ci: (ci, 0)),
                  pl.BlockSpec((lc, half), lambda bi, ci: (ci, 0)),
                  pl.BlockSpec((half, lc), lambda bi, ci: (0, ci)),
                  pl.BlockSpec((half, lc), lambda bi, ci: (0, ci)),
                  _const((RET_HEADS, lc, lc)), _const((RET_HEADS, lc, 1)),
                  _const((RET_HEADS, 1, lc)), _const((RET_HEADS, 1, 1)), state],
        out_specs=[pl.BlockSpec((lc, e), rowc), state],
        out_shape=[jax.ShapeDtypeStruct((b * t_len, e), F32),
                   jax.ShapeDtypeStruct((b, RET_HEADS, RET_DK, RET_DV), F32)],
        scratch_shapes=[pltpu.VMEM((RET_HEADS, RET_DK, RET_DV), F32)],
        compiler_params=_params("parallel", "arbitrary"),
        name="ret_core",
    )(q, kt, v, tabs["cos"], tabs["sin"], tabs["cos_t"], tabs["sin_t"], tabs["mask"],
      tabs["qdec"], tabs["kdec"], tabs["gl"], s0)


def _ret_tables(pos, lc):
    half = RET_DK // 2
    inv = ROPE_BASE ** (-jnp.arange(half, dtype=F32) / half)
    ang = pos.astype(F32)[:, None] * inv[None, :]
    cos = jnp.cos(ang)
    sin = jnp.sin(ang)
    log_g = jnp.log1p(-jnp.exp2(-5.0 - jnp.arange(RET_HEADS, dtype=F32)))
    n = jnp.arange(lc, dtype=F32)
    diff = n[:, None] - n[None, :]
    mask = jnp.where(diff[None] >= 0, jnp.exp(diff[None] * log_g[:, None, None]), 0.0)
    qdec = jnp.exp((n[None, :] + 1.0) * log_g[:, None])[:, :, None]
    kdec = jnp.exp((lc - 1.0 - n)[None, :] * log_g[:, None])[:, None, :]
    gl = jnp.exp(lc * log_g)[:, None, None]
    return dict(cos=cos, sin=sin, cos_t=cos.T, sin_t=sin.T, mask=mask, qdec=qdec, kdec=kdec, gl=gl)


def _s5_block_weights(bb_re, bb_im, c_re, c_im):
    gl = S5_GROUPS // S5_KB
    eye = jnp.eye(gl, dtype=F32)

    def bdiag_b(bb):
        x = bb.reshape(S5_KB, gl, S5_GROUP, S5_STATE)
        x = x[:, :, :, None, :] * eye[None, :, None, :, None]
        return x.reshape(S5_KB, gl * S5_GROUP, gl * S5_STATE)

    def bdiag_c(c):
        x = jnp.swapaxes(c, 1, 2).reshape(S5_KB, gl, S5_STATE, S5_GROUP)
        x = x[:, :, :, None, :] * eye[None, :, None, :, None]
        return x.reshape(S5_KB, gl * S5_STATE, gl * S5_GROUP)

    bw = jnp.concatenate([bdiag_b(bb_re), bdiag_b(bb_im)], axis=2).astype(BF16)
    cw = jnp.concatenate([bdiag_c(c_re), -bdiag_c(c_im)], axis=1).astype(BF16)
    return bw, cw


def _per_problem(param, nbatch):
    x = param.reshape(RWKV_HEADS, RWKV_HEAD).T
    return jnp.tile(x, (1, nbatch))


def _to_problem_layout(x, t_len, nb):
    x = x.reshape(t_len, nb, RWKV_HEADS, RWKV_HEAD)
    return jnp.transpose(x, (0, 3, 1, 2)).reshape(t_len, RWKV_HEAD, nb * RWKV_HEADS)


def _from_problem_layout(x, t_len, nb):
    x = x.reshape(t_len, RWKV_HEAD, nb, RWKV_HEADS)
    return jnp.transpose(x, (0, 2, 3, 1)).reshape(t_len * nb, D_INNER)


def _run_stream(x, states, wts, norm_pre, norm_post, *, nb, t_len, pos0, tiles):
    d = D_MODEL
    e = D_INNER
    new = {}
    gp = lambda i: (norm_pre[i].reshape(1, d), norm_post[i].reshape(1, d))

    gpre, gpost = gp(0)
    cs = jnp.transpose(states["lru_conv"], (1, 0, 2)).reshape((LRU_CONV_W - 1) * nb, e)
    x, cso, ho = _lru_layer(x, cs, states["lru_h"], gpre, gpost, wts["lru"], nb=nb, tt=tiles["lru_tt"])
    new["lru_conv"] = jnp.transpose(cso.reshape(LRU_CONV_W - 1, nb, e), (1, 0, 2))
    new["lru_h"] = ho

    gpre, gpost = gp(1)
    w = wts["s5"]
    gn = S5_GROUPS * S5_STATE
    uz = _norm_proj(x, gpre, w["w_in"], rows=tiles["rows"])
    y, sre, sim = _s5_core(uz, states["s5_re"].reshape(nb, gn), states["s5_im"].reshape(nb, gn),
                           w["bw"], w["cw"], w["ab_re"], w["ab_im"], w["d"], nb=nb, tt=tiles["s5_tt"])
    x = _s5_out(y, uz, x, w["glu_w"], w["glu_b"], w["w_out"], gpost, rows=tiles["rows"])
    new["s5_re"] = sre.reshape(nb, S5_GROUPS, S5_STATE)
    new["s5_im"] = sim.reshape(nb, S5_GROUPS, S5_STATE)

    gpre, gpost = gp(2)
    w = wts["rwkv"]
    r, k, v, g, dec, a, shift = _rwkv_proj(x, states["rwkv_shift"], gpre, w, nb=nb, tt=tiles["rwkv_tt"])
    tp = functools.partial(_to_problem_layout, t_len=t_len, nb=nb)
    s0 = jnp.transpose(states["rwkv_wkv"], (3, 2, 0, 1)).reshape(RWKV_HEAD, RWKV_HEAD, nb * RWKV_HEADS)
    pk = {n: _per_problem(w[n], nb) for n in ("k_k", "k_a", "r_k", "ln_w", "ln_b")}
    o, s_new = _rwkv_rec(tp(r), tp(dec), tp(k), tp(v), tp(a), pk, s0, tt=tiles["rec_tt"])
    o = _from_problem_layout(o, t_len, nb)
    x = _gated_out(o, g, x, w["w_o"], gpost, rows=tiles["rows"])
    new["rwkv_shift"] = shift
    new["rwkv_wkv"] = jnp.transpose(s_new.reshape(RWKV_HEAD, RWKV_HEAD, nb, RWKV_HEADS), (2, 3, 1, 0))

    gpre, gpost = gp(3)
    w = wts["ret"]
    lc = tiles["ret_lc"]
    t_pad = -(-t_len // lc) * lc
    pad = t_pad - t_len
    xb = jnp.transpose(x.reshape(t_len, nb, d), (1, 0, 2))
    xb = jnp.pad(xb, ((0, 0), (pad, 0), (0, 0))).reshape(nb * t_pad, d)
    q, kk, v, g = _ret_proj(xb, gpre, w, rows=tiles["ret_rows"])
    kt = jnp.transpose(kk.reshape(nb, t_pad, RET_HEADS * RET_DK), (0, 2, 1))
    tabs = _ret_tables(pos0 - pad + jnp.arange(t_pad, dtype=jnp.int32), lc)
    y, s_new = _ret_core(q, kt, v, tabs, states["ret"], lc=lc)
    xb = _gated_out(y, g, xb, w["w_o"], gpost, rows=tiles["ret_rows"])
    new["ret"] = s_new
    xb = xb.reshape(nb, t_pad, d)[:, pad:]
    return xb, new


def kernel(x_prompt, x_sample, state_lru_conv, state_lru_h, state_s5_re, state_s5_im, state_rwkv_shift, state_rwkv_wkv, state_ret, meta_tokens, norm_pre, norm_post, lru_w_in, lru_conv_w, lru_conv_b, lru_wa, lru_ba, lru_wx, lru_bx, lru_lam, lru_w_out, s5_w_in, s5_log_dt, s5_a_re, s5_a_im, s5_b_re, s5_b_im, s5_c_re, s5_c_im, s5_d, s5_glu_w, s5_glu_b, s5_w_out, rwkv_mu, rwkv_w_r, rwkv_w_k, rwkv_w_v, rwkv_w_g, rwkv_w0, rwkv_w1, rwkv_w2, rwkv_a0, rwkv_a1, rwkv_a2, rwkv_k_k, rwkv_k_a, rwkv_r_k, rwkv_ln_w, rwkv_ln_b, rwkv_w_o, ret_w_q, ret_w_k, ret_w_v, ret_w_g, ret_w_o):
    d = D_MODEL
    e = D_INNER
    bp, seq, _ = x_prompt.shape
    bs, dec_seq, _ = x_sample.shape
    t_p = seq + N_META
    bf = lambda x: x.astype(BF16)
    vec = lambda x: x.reshape(1, -1)

    ab_re, ab_im, bb_re, bb_im = _s5_discretize(
        s5_log_dt[0], s5_a_re[0], s5_a_im[0],
        jnp.swapaxes(s5_b_re[0], 1, 2), jnp.swapaxes(s5_b_im[0], 1, 2))
    bw, cw = _s5_block_weights(bb_re, bb_im, s5_c_re[0], s5_c_im[0])
    wts = dict(
        lru=dict(w_in=bf(lru_w_in[0]), conv_w=lru_conv_w[0], conv_b=vec(lru_conv_b[0]),
                 wa=bf(lru_wa[0]), ba=vec(lru_ba[0]), wx=bf(lru_wx[0]), bx=vec(lru_bx[0]),
                 lam=vec(lru_lam[0]), w_out=bf(lru_w_out[0])),
        s5=dict(w_in=bf(s5_w_in[0]), bw=bw, cw=cw, ab_re=vec(ab_re), ab_im=vec(ab_im),
                d=vec(s5_d[0]), glu_w=bf(s5_glu_w[0]), glu_b=vec(s5_glu_b[0]), w_out=bf(s5_w_out[0])),
        rwkv=dict(mu=rwkv_mu[0], w_r=bf(rwkv_w_r[0]), w_k=bf(rwkv_w_k[0]), w_v=bf(rwkv_w_v[0]),
                  w_g=bf(rwkv_w_g[0]), w0=vec(rwkv_w0[0]), w1=bf(rwkv_w1[0]), w2=bf(rwkv_w2[0]),
                  a0=vec(rwkv_a0[0]), a1=bf(rwkv_a1[0]), a2=bf(rwkv_a2[0]),
                  k_k=rwkv_k_k[0], k_a=rwkv_k_a[0], r_k=rwkv_r_k[0], ln_w=rwkv_ln_w[0],
                  ln_b=rwkv_ln_b[0], w_o=bf(rwkv_w_o[0])),
        ret=dict(w_q=bf(ret_w_q[0]), w_k=bf(ret_w_k[0]), w_v=bf(ret_w_v[0]), w_g=bf(ret_w_g[0]),
                 w_o=bf(ret_w_o[0])),
    )

    meta = jnp.broadcast_to(meta_tokens[None].astype(x_prompt.dtype), (bp, N_META, d))
    hp = jnp.concatenate([meta, x_prompt], axis=1)
    xp = jnp.transpose(hp, (1, 0, 2)).reshape(t_p * bp, d)
    z = lambda *s: jnp.zeros(s, F32)
    p_states = dict(lru_conv=z(bp, LRU_CONV_W - 1, e), lru_h=z(bp, e),
                    s5_re=z(bp, S5_GROUPS, S5_STATE), s5_im=z(bp, S5_GROUPS, S5_STATE),
                    rwkv_shift=z(bp, d), rwkv_wkv=z(bp, RWKV_HEADS, RWKV_HEAD, RWKV_HEAD),
                    ret=z(bp, RET_HEADS, RET_DK, RET_DV))
    p_tiles = dict(lru_tt=_divisor_tile(t_p, 48, 1), s5_tt=_divisor_tile(t_p, 16, 1),
                   rwkv_tt=_divisor_tile(t_p, 16, 1), rec_tt=_divisor_tile(t_p, 16, 1),
                   rows=_divisor_tile(t_p * bp, 384, 8), ret_lc=RET_CHUNK,
                   ret_rows=_divisor_tile(bp * (-(-t_p // RET_CHUNK) * RET_CHUNK), 512, 8))
    yp, new_p = _run_stream(xp, p_states, wts, norm_pre, norm_post, nb=bp, t_len=t_p, pos0=0,
                            tiles=p_tiles)

    xs = jnp.transpose(x_sample, (1, 0, 2)).reshape(dec_seq * bs, d)
    s_states = dict(lru_conv=state_lru_conv[0], lru_h=state_lru_h[0], s5_re=state_s5_re[0],
                    s5_im=state_s5_im[0], rwkv_shift=state_rwkv_shift[0],
                    rwkv_wkv=state_rwkv_wkv[0], ret=state_ret[0])
    s_rows = _divisor_tile(dec_seq * bs, 256, 8)
    s_tiles = dict(lru_tt=2, s5_tt=1, rwkv_tt=2, rec_tt=dec_seq, rows=s_rows, ret_lc=dec_seq,
                   ret_rows=s_rows)
    ys, new_s = _run_stream(xs, s_states, wts, norm_pre, norm_post, nb=bs, t_len=dec_seq,
                            pos0=PAST_LEN, tiles=s_tiles)

    names = ("lru_conv", "lru_h", "s5_re", "s5_im", "rwkv_shift", "rwkv_wkv", "ret")
    return ((yp[:, N_META:], ys)
            + tuple(new_p[n][None] for n in names)
            + tuple(new_s[n][None] for n in names))
```

```python
import functools
import math

import jax
import jax.numpy as jnp
from jax import lax
from jax.experimental import pallas as pl
from jax.experimental.pallas import tpu as pltpu

F32 = jnp.float32
BF16 = jnp.bfloat16

D_MODEL = 1024
D_INNER = 2048
N_META = 16
NORM_EPS = 1e-6
LRU_CONV_W = 4
LRU_BLOCKS = 16
LRU_BLOCK = 128
LRU_C = 8.0
S5_GROUP = 16
S5_GROUPS = 128
S5_STATE = 64
S5_KB = 8
RWKV_HEAD = 64
RWKV_HEADS = 32
RWKV_LN_EPS = 64e-5
RET_HEADS = 4
RET_DK = 256
RET_DV = 512
ROPE_BASE = 10000.0
PAST_LEN = 16384
RET_CHUNK = 128

LANES = 128
VMEM_LIMIT = 56 * 1024 * 1024


def _params(*sem):
    return pltpu.CompilerParams(dimension_semantics=sem, vmem_limit_bytes=VMEM_LIMIT)


def _const(shape):
    zeros = (0,) * len(shape)
    return pl.BlockSpec(shape, lambda *_: zeros)


def _divisor_tile(n, pref, mult):
    best = mult
    t = mult
    while t <= min(n, pref):
        if n % t == 0:
            best = t
        t += mult
    assert n % best == 0
    return best


def _rms(x, g):
    return x * lax.rsqrt(jnp.mean(x * x, axis=-1, keepdims=True) + NORM_EPS) * g


def _mm(a, w):
    return jnp.dot(a.astype(BF16), w, preferred_element_type=F32)


def _silu(x):
    return x * jax.nn.sigmoid(x)


def _softplus(x):
    return jnp.maximum(x, 0.0) + jnp.log1p(jnp.exp(-jnp.abs(x)))


def _lru_kernel(x_ref, cs_ref, h0_ref, gpre_ref, gpost_ref, win_ref, cw_ref, cb_ref,
                wa_ref, ba_ref, wx_ref, bx_ref, lam_ref, wout_ref,
                xo_ref, cso_ref, ho_ref,
                uz_ref, a_ref, b_ref, tail_ref, h_ref, *, nb, tt):
    rows = nb * tt
    e = D_INNER

    @pl.when(pl.program_id(0) == 0)
    def _():
        tail_ref[...] = cs_ref[...]
        h_ref[...] = h0_ref[...]

    xn = _rms(x_ref[...], gpre_ref[...])
    uz_ref[...] = _mm(xn, win_ref[...])

    def block(j, carry):
        sl = pl.ds(pl.multiple_of(j * LRU_BLOCK, LRU_BLOCK), LRU_BLOCK)
        ext = jnp.concatenate([tail_ref[:, sl], uz_ref[:, sl]], axis=0)
        cw = cw_ref[:, sl]
        xc = cb_ref[:, sl]
        for jj in range(LRU_CONV_W):
            xc = xc + ext[jj * nb:jj * nb + rows] * cw[jj:jj + 1]
        tail_ref[:, sl] = ext[rows:rows + (LRU_CONV_W - 1) * nb]
        gate_r = jax.nn.sigmoid(_mm(xc, wa_ref[j]) + ba_ref[:, sl])
        gate_i = jax.nn.sigmoid(_mm(xc, wx_ref[j]) + bx_ref[:, sl])
        log_a = -LRU_C * gate_r * _softplus(-lam_ref[:, sl])
        a = jnp.exp(log_a)
        a_ref[:, sl] = a
        b_ref[:, sl] = jnp.sqrt(-jnp.tanh(log_a) * (a * a + 1.0)) * gate_i * xc
        return carry

    lax.fori_loop(0, LRU_BLOCKS, block, 0)

    def step(t, carry):
        r = pl.ds(pl.multiple_of(t * nb, nb), nb)
        h = a_ref[r, :] * h_ref[...] + b_ref[r, :]
        h_ref[...] = h
        b_ref[r, :] = h
        return carry

    lax.fori_loop(0, tt, step, 0)

    y = b_ref[...] * _silu(uz_ref[:, e:])
    out = _mm(y, wout_ref[...])
    xo_ref[...] = x_ref[...] + _rms(out, gpost_ref[...])
    cso_ref[...] = tail_ref[...]
    ho_ref[...] = h_ref[...]


def _lru_layer(x, conv_state, h0, gpre, gpost, w, *, nb, tt):
    r, d = x.shape
    e = D_INNER
    rows = nb * tt
    ctail = (LRU_CONV_W - 1) * nb
    row = lambda i: (i, 0)
    return pl.pallas_call(
        functools.partial(_lru_kernel, nb=nb, tt=tt),
        grid=(r // rows,),
        in_specs=[
            pl.BlockSpec((rows, d), row), _const((ctail, e)), _const((nb, e)),
            _const((1, d)), _const((1, d)), _const((d, 2 * e)),
            _const((LRU_CONV_W, e)), _const((1, e)),
            _const((LRU_BLOCKS, LRU_BLOCK, LRU_BLOCK)), _const((1, e)),
            _const((LRU_BLOCKS, LRU_BLOCK, LRU_BLOCK)), _const((1, e)),
            _const((1, e)), _const((e, d)),
        ],
        out_specs=[pl.BlockSpec((rows, d), row), _const((ctail, e)), _const((nb, e))],
        out_shape=[jax.ShapeDtypeStruct((r, d), F32),
                   jax.ShapeDtypeStruct((ctail, e), F32),
                   jax.ShapeDtypeStruct((nb, e), F32)],
        scratch_shapes=[pltpu.VMEM((rows, 2 * e), F32), pltpu.VMEM((rows, e), F32),
                        pltpu.VMEM((rows, e), F32), pltpu.VMEM((ctail, e), F32),
                        pltpu.VMEM((nb, e), F32)],
        compiler_params=_params("arbitrary"),
        name="lru_layer",
    )(x, conv_state, h0, gpre, gpost, w["w_in"], w["conv_w"], w["conv_b"],
      w["wa"], w["ba"], w["wx"], w["bx"], w["lam"], w["w_out"])


def _s5_disc_kernel(logdt_ref, are_ref, aim_ref, bre_ref, bim_ref,
                    abre_ref, abim_ref, bbre_ref, bbim_ref):
    dt = jnp.exp(logdt_ref[...])
    a_re = are_ref[...]
    a_im = aim_ref[...]
    mag = jnp.exp(dt * a_re)
    ang = dt * a_im
    ab_re = mag * jnp.cos(ang)
    ab_im = mag * jnp.sin(ang)
    den = a_re * a_re + a_im * a_im
    f_re = ((ab_re - 1.0) * a_re + ab_im * a_im) / den
    f_im = (ab_im * a_re - (ab_re - 1.0) * a_im) / den
    abre_ref[...] = ab_re
    abim_ref[...] = ab_im
    b_re = bre_ref[...]
    b_im = bim_ref[...]
    fr = f_re[:, None, :]
    fi = f_im[:, None, :]
    bbre_ref[...] = fr * b_re - fi * b_im
    bbim_ref[...] = fr * b_im + fi * b_re


def _s5_discretize(log_dt, a_re, a_im, b_re, b_im):
    g, n = a_re.shape
    c = b_re.shape[1]
    return pl.pallas_call(
        _s5_disc_kernel,
        out_shape=[jax.ShapeDtypeStruct((g, n), F32), jax.ShapeDtypeStruct((g, n), F32),
                   jax.ShapeDtypeStruct((g, c, n), F32), jax.ShapeDtypeStruct((g, c, n), F32)],
        name="s5_discretize",
    )(log_dt.reshape(g, 1), a_re, a_im, b_re, b_im)


def _proj_kernel(x_ref, g_ref, w_ref, o_ref):
    o_ref[...] = _mm(_rms(x_ref[...], g_ref[...]), w_ref[...])


def _norm_proj(x, g, w, *, rows):
    r, d = x.shape
    n = w.shape[1]
    row = lambda i: (i, 0)
    return pl.pallas_call(
        _proj_kernel,
        grid=(r // rows,),
        in_specs=[pl.BlockSpec((rows, d), row), _const((1, d)), _const((d, n))],
        out_specs=pl.BlockSpec((rows, n), row),
        out_shape=jax.ShapeDtypeStruct((r, n), F32),
        compiler_params=_params("parallel"),
        name="norm_proj",
    )(x, g, w)


def _gelu_tanh(x):
    return 0.5 * x * (1.0 + jnp.tanh(math.sqrt(2.0 / math.pi) * (x + 0.044715 * (x * x * x))))


def _s5_core_kernel(u_ref, sre_ref, sim_ref, bw_ref, cw_ref, abre_ref, abim_ref, d_ref,
                    y_ref, sreo_ref, simo_ref,
                    bur_ref, bui_ref, xr_ref, xi_ref, *, nb, tt):
    gn = S5_GROUPS * S5_STATE // S5_KB
    ch = D_INNER // S5_KB

    @pl.when(pl.program_id(0) == 0)
    def _():
        xr_ref[...] = sre_ref[...]
        xi_ref[...] = sim_ref[...]

    for kb in range(S5_KB):
        bu = _mm(u_ref[:, kb * ch:(kb + 1) * ch], bw_ref[kb])
        bur_ref[:, kb * gn:(kb + 1) * gn] = bu[:, :gn]
        bui_ref[:, kb * gn:(kb + 1) * gn] = bu[:, gn:]

    def step(t, carry):
        r = pl.ds(pl.multiple_of(t * nb, nb), nb)
        ar = abre_ref[...]
        ai = abim_ref[...]
        xr = xr_ref[...]
        xi = xi_ref[...]
        nr = ar * xr - ai * xi + bur_ref[r, :]
        ni = ar * xi + ai * xr + bui_ref[r, :]
        xr_ref[...] = nr
        xi_ref[...] = ni
        bur_ref[r, :] = nr
        bui_ref[r, :] = ni
        return carry

    lax.fori_loop(0, tt, step, 0)

    for kb in range(S5_KB):
        y = (_mm(bur_ref[:, kb * gn:(kb + 1) * gn], cw_ref[kb, :gn, :])
             + _mm(bui_ref[:, kb * gn:(kb + 1) * gn], cw_ref[kb, gn:, :]))
        y = y + d_ref[:, kb * ch:(kb + 1) * ch] * u_ref[:, kb * ch:(kb + 1) * ch]
        y_ref[:, kb * ch:(kb + 1) * ch] = _gelu_tanh(y)

    sreo_ref[...] = xr_ref[...]
    simo_ref[...] = xi_ref[...]


def _s5_core(uz, s_re, s_im, bw, cw, ab_re, ab_im, dvec, *, nb, tt):
    r = uz.shape[0]
    e = D_INNER
    gn = S5_GROUPS * S5_STATE
    rows = nb * tt
    row = lambda i: (i, 0)
    return pl.pallas_call(
        functools.partial(_s5_core_kernel, nb=nb, tt=tt),
        grid=(r // rows,),
        in_specs=[pl.BlockSpec((rows, e), row), _const((nb, gn)), _const((nb, gn)),
                  _const(bw.shape), _const(cw.shape), _const((1, gn)), _const((1, gn)),
                  _const((1, e))],
        out_specs=[pl.BlockSpec((rows, e), row), _const((nb, gn)), _const((nb, gn))],
        out_shape=[jax.ShapeDtypeStruct((r, e), F32), jax.ShapeDtypeStruct((nb, gn), F32),
                   jax.ShapeDtypeStruct((nb, gn), F32)],
        scratch_shapes=[pltpu.VMEM((rows, gn), F32), pltpu.VMEM((rows, gn), F32),
                        pltpu.VMEM((nb, gn), F32), pltpu.VMEM((nb, gn), F32)],
        compiler_params=_params("arbitrary"),
        name="s5_core",
    )(uz, s_re, s_im, bw, cw, ab_re, ab_im, dvec)


def _s5_out_kernel(y_ref, gate_ref, x_ref, gluw_ref, glub_ref, wout_ref, gpost_ref, xo_ref):
    y = y_ref[...]
    y = y * jax.nn.sigmoid(_mm(y, gluw_ref[...]) + glub_ref[...])
    y = y * _silu(gate_ref[...])
    xo_ref[...] = x_ref[...] + _rms(_mm(y, wout_ref[...]), gpost_ref[...])


def _s5_out(y, uz, x, glu_w, glu_b, w_out, gpost, *, rows):
    r, d = x.shape
    e = D_INNER
    row = lambda i: (i, 0)
    return pl.pallas_call(
        _s5_out_kernel,
        grid=(r // rows,),
        in_specs=[pl.BlockSpec((rows, e), row), pl.BlockSpec((rows, e), lambda i: (i, 1)),
                  pl.BlockSpec((rows, d), row), _const((e, e)), _const((1, e)),
                  _const((e, d)), _const((1, d))],
        out_specs=pl.BlockSpec((rows, d), row),
        out_shape=jax.ShapeDtypeStruct((r, d), F32),
        compiler_params=_params("parallel"),
        name="s5_out",
    )(y, uz, x, glu_w, glu_b, w_out, gpost)


def _rwkv_proj_kernel(x_ref, xprev_ref, gpre_ref, mu_ref, wr_ref, wk_ref, wv_ref, wg_ref,
                      w0_ref, w1_ref, w2_ref, a0_ref, a1_ref, a2_ref,
                      r_ref, k_ref, v_ref, g_ref, dec_ref, a_ref, shift_ref,
                      prev_ref, *, nb, tt):
    rows = nb * tt

    @pl.when(pl.program_id(0) == 0)
    def _():
        prev_ref[...] = xprev_ref[...]

    xn = _rms(x_ref[...], gpre_ref[...])
    if tt > 1:
        shifted = jnp.concatenate([prev_ref[...], xn[:rows - nb]], axis=0)
    else:
        shifted = prev_ref[...]
    prev_ref[...] = xn[rows - nb:]
    shift_ref[...] = xn[rows - nb:]
    xx = shifted - xn
    mix = lambda n: xn + xx * mu_ref[n:n + 1, :]
    r_ref[...] = _mm(mix(0), wr_ref[...])
    k_ref[...] = _mm(mix(2), wk_ref[...])
    v_ref[...] = _mm(mix(3), wv_ref[...])
    g_ref[...] = _silu(_mm(mix(5), wg_ref[...]))
    w_raw = w0_ref[...] + _mm(jnp.tanh(_mm(mix(1), w1_ref[...])), w2_ref[...])
    dec_ref[...] = jnp.exp(-(jax.nn.sigmoid(w_raw) * math.exp(-0.5)))
    a_ref[...] = jax.nn.sigmoid(a0_ref[...] + _mm(_mm(mix(4), a1_ref[...]), a2_ref[...]))


def _rwkv_proj(x, x_prev, gpre, w, *, nb, tt):
    r, d = x.shape
    e = D_INNER
    rows = nb * tt
    lora = w["w1"].shape[1]
    row = lambda i: (i, 0)
    big = pl.BlockSpec((rows, e), row)
    return pl.pallas_call(
        functools.partial(_rwkv_proj_kernel, nb=nb, tt=tt),
        grid=(r // rows,),
        in_specs=[pl.BlockSpec((rows, d), row), _const((nb, d)), _const((1, d)), _const((6, d)),
                  _const((d, e)), _const((d, e)), _const((d, e)), _const((d, e)),
                  _const((1, e)), _const((d, lora)), _const((lora, e)),
                  _const((1, e)), _const((d, lora)), _const((lora, e))],
        out_specs=[big, big, big, big, big, big, _const((nb, d))],
        out_shape=[jax.ShapeDtypeStruct((r, e), F32)] * 6 + [jax.ShapeDtypeStruct((nb, d), F32)],
        scratch_shapes=[pltpu.VMEM((nb, d), F32)],
        compiler_params=_params("arbitrary"),
        name="rwkv_proj",
    )(x, x_prev, gpre, w["mu"], w["w_r"], w["w_k"], w["w_v"], w["w_g"],
      w["w0"], w["w1"], w["w2"], w["a0"], w["a1"], w["a2"])


RWKV_VSUB = 4
RWKV_VGRP = RWKV_HEAD // RWKV_VSUB
RWKV_BG = 8
RWKV_VBLK = 8
RWKV_UNROLL = 8


def _seg_sum(x):
    lane_axis = x.ndim - 1
    x = x + pltpu.roll(x, 2 * RWKV_HEADS, lane_axis)
    return x + pltpu.roll(x, RWKV_HEADS, lane_axis)


def _rwkv_rec_kernel(r_ref, w_ref, k_ref, v_ref, a_ref, kk_ref, ka_ref, rk_ref, lnw_ref, lnb_ref,
                     s0_ref, o_ref, so_ref, s_ref, vec_ref, sa_scale_ref, *, tt):
    n = RWKV_HEAD
    nh = RWKV_HEADS
    shape = (RWKV_BG, LANES)

    @pl.when(pl.program_id(1) == 0)
    def _():
        s_ref[...] = s0_ref[0]

    tiles = [slice(j * LANES, (j + 1) * LANES) for j in range(RWKV_VGRP)]
    seg = lax.broadcasted_iota(jnp.int32, (tt,) + shape, 2) // nh
    seg_low = seg < 2
    seg_odd = (seg & 1) == 1

    def rep4(x):
        rolled = [x] + [pltpu.roll(x, q * nh, 2) for q in range(1, RWKV_VSUB)]
        pair = [jnp.where(seg_low, rolled[i], rolled[(i + 2) % RWKV_VSUB]) for i in range(RWKV_VSUB)]
        return [jnp.where(seg_odd, pair[(1 - s_) % RWKV_VSUB], pair[(-s_) % RWKV_VSUB])
                for s_ in range(RWKV_VSUB)]

    n2 = jnp.zeros((tt,) + shape, F32)
    bon = jnp.zeros((tt,) + shape, F32)
    for j, sl in enumerate(tiles):
        r = r_ref[:, :, sl]
        k = k_ref[:, :, sl]
        a = a_ref[:, :, sl]
        kk = k * kk_ref[:, sl]
        k2 = k * (1.0 + (a - 1.0) * ka_ref[:, sl])
        n2 = n2 + kk * kk
        bon = bon + r * k2 * rk_ref[:, sl]
        for idx, x in enumerate((r, w_ref[:, :, sl], kk, kk * a, k2)):
            for s_, rep in enumerate(rep4(x)):
                vec_ref[:, idx, RWKV_VSUB * j + s_] = rep
    inv = lax.rsqrt(jnp.maximum(_seg_sum(n2), 1e-24))
    sa_scale_ref[...] = -(inv * inv)
    bon = _seg_sum(bon)

    def token(t, carry):
        sa_scale = sa_scale_ref[t]
        zero = jnp.zeros((RWKV_VBLK,) + shape, F32)
        for vb in range(RWKV_VGRP // RWKV_VBLK):
            vsl = slice(vb * RWKV_VBLK, (vb + 1) * RWKV_VBLK)
            vv = jnp.stack([v_ref[t, :, sl] for sl in tiles[vsl]])

            def sa_step(c, acc):
                return acc + s_ref[c, vsl] * vec_ref[t, 2, c]

            sa = lax.fori_loop(0, n, sa_step, zero, unroll=RWKV_UNROLL) * sa_scale

            def update_step(c, acc):
                s_new = s_ref[c, vsl] * vec_ref[t, 1, c] + sa * vec_ref[t, 3, c] + vv * vec_ref[t, 4, c]
                s_ref[c, vsl] = s_new
                return acc + s_new * vec_ref[t, 0, c]

            y = lax.fori_loop(0, n, update_step, zero, unroll=RWKV_UNROLL)
            for i, sl in enumerate(tiles[vsl]):
                o_ref[t, :, sl] = y[i]
        return carry

    lax.fori_loop(0, tt, token, 0)

    tot = o_ref[:, :, tiles[0]]
    for sl in tiles[1:]:
        tot = tot + o_ref[:, :, sl]
    mean = _seg_sum(tot) * (1.0 / n)
    sq = jnp.zeros((tt,) + shape, F32)
    for sl in tiles:
        yc = o_ref[:, :, sl] - mean
        sq = sq + yc * yc
    rstd = lax.rsqrt(_seg_sum(sq) * (1.0 / n) + RWKV_LN_EPS)
    for sl in tiles:
        o_ref[:, :, sl] = ((o_ref[:, :, sl] - mean) * rstd * lnw_ref[:, sl] + lnb_ref[:, sl]
                           + bon * v_ref[:, :, sl])

    @pl.when(pl.program_id(1) == pl.num_programs(1) - 1)
    def _():
        so_ref[0] = s_ref[...]


def _rwkv_rec(r, w, k, v, a, pk, s0, *, tt):
    t_len, nb, e = r.shape
    n = RWKV_HEAD
    tok = pl.BlockSpec((tt, RWKV_BG, e), lambda bi, ti: (ti, bi, 0))
    st = pl.BlockSpec((1, n, RWKV_VGRP, RWKV_BG, LANES), lambda bi, ti: (bi, 0, 0, 0, 0))
    return pl.pallas_call(
        functools.partial(_rwkv_rec_kernel, tt=tt),
        grid=(nb // RWKV_BG, t_len // tt),
        in_specs=[tok] * 5 + [_const((RWKV_BG, e))] * 5 + [st],
        out_specs=[tok, st],
        out_shape=[jax.ShapeDtypeStruct((t_len, nb, e), F32), jax.ShapeDtypeStruct(s0.shape, F32)],
        scratch_shapes=[pltpu.VMEM((n, RWKV_VGRP, RWKV_BG, LANES), F32),
                        pltpu.VMEM((tt, 5, n, RWKV_BG, LANES), F32),
                        pltpu.VMEM((tt, RWKV_BG, LANES), F32)],
        compiler_params=_params("parallel", "arbitrary"),
        name="rwkv_recurrence",
    )(r, w, k, v, a, pk["k_k"], pk["k_a"], pk["r_k"], pk["ln_w"], pk["ln_b"], s0)


def _gated_out_kernel(y_ref, g_ref, x_ref, w_ref, gpost_ref, xo_ref):
    y = y_ref[...] * g_ref[...]
    xo_ref[...] = x_ref[...] + _rms(_mm(y, w_ref[...]), gpost_ref[...])


def _gated_out(y, g, x, w, gpost, *, rows):
    r, d = x.shape
    e = D_INNER
    row = lambda i: (i, 0)
    return pl.pallas_call(
        _gated_out_kernel,
        grid=(r // rows,),
        in_specs=[pl.BlockSpec((rows, e), row), pl.BlockSpec((rows, e), row),
                  pl.BlockSpec((rows, d), row), _const((e, d)), _const((1, d))],
        out_specs=pl.BlockSpec((rows, d), row),
        out_shape=jax.ShapeDtypeStruct((r, d), F32),
        compiler_params=_params("parallel"),
        name="gated_out",
    )(y, g, x, w, gpost)


def _ret_proj_kernel(x_ref, gpre_ref, wq_ref, wk_ref, wv_ref, wg_ref, q_ref, k_ref, v_ref, g_ref):
    xn = _rms(x_ref[...], gpre_ref[...]).astype(BF16)
    q_ref[...] = jnp.dot(xn, wq_ref[...], preferred_element_type=F32)
    k_ref[...] = jnp.dot(xn, wk_ref[...], preferred_element_type=F32)
    v_ref[...] = jnp.dot(xn, wv_ref[...], preferred_element_type=F32)
    g_ref[...] = _silu(jnp.dot(xn, wg_ref[...], preferred_element_type=F32))


def _ret_proj(x, gpre, w, *, rows):
    r, d = x.shape
    e = D_INNER
    row = lambda i: (i, 0)
    return pl.pallas_call(
        _ret_proj_kernel,
        grid=(r // rows,),
        in_specs=[pl.BlockSpec((rows, d), row), _const((1, d)), _const((d, d)), _const((d, d)),
                  _const((d, e)), _const((d, e))],
        out_specs=[pl.BlockSpec((rows, d), row), pl.BlockSpec((rows, d), row),
                   pl.BlockSpec((rows, e), row), pl.BlockSpec((rows, e), row)],
        out_shape=[jax.ShapeDtypeStruct((r, d), F32), jax.ShapeDtypeStruct((r, d), F32),
                   jax.ShapeDtypeStruct((r, e), F32), jax.ShapeDtypeStruct((r, e), F32)],
        compiler_params=_params("parallel"),
        name="ret_proj",
    )(x, gpre, w["w_q"], w["w_k"], w["w_v"], w["w_g"])


def _ret_core_kernel(q_ref, kt_ref, v_ref, cos_ref, sin_ref, cost_ref, sint_ref, mask_ref,
                     qdec_ref, kdec_ref, gl_ref, s0_ref, y_ref, so_ref, s_ref):
    half = RET_DK // 2

    @pl.when(pl.program_id(1) == 0)
    def _():
        s_ref[...] = s0_ref[0]

    cos = cos_ref[...]
    sin = sin_ref[...]
    cost = cost_ref[...]
    sint = sint_ref[...]
    for h in range(RET_HEADS):
        q1 = q_ref[:, h * RET_DK:h * RET_DK + half]
        q2 = q_ref[:, h * RET_DK + half:(h + 1) * RET_DK]
        qr = jnp.concatenate([q1 * cos - q2 * sin, q2 * cos + q1 * sin], axis=1)
        k1 = kt_ref[0, h * RET_DK:h * RET_DK + half, :]
        k2 = kt_ref[0, h * RET_DK + half:(h + 1) * RET_DK, :]
        krt = jnp.concatenate([k1 * cost - k2 * sint, k2 * cost + k1 * sint], axis=0)
        krt = krt * (RET_DK ** -0.5)
        vh = v_ref[:, h * RET_DV:(h + 1) * RET_DV].astype(BF16)
        qb = qr.astype(BF16)
        scores = jnp.dot(qb, krt.astype(BF16), preferred_element_type=F32) * mask_ref[h]
        s_old = s_ref[h]
        y = jnp.dot(scores.astype(BF16), vh, preferred_element_type=F32)
        y = y + jnp.dot(qb, s_old.astype(BF16), preferred_element_type=F32) * qdec_ref[h]
        kw = (krt * kdec_ref[h]).astype(BF16)
        s_ref[h] = gl_ref[h] * s_old + jnp.dot(kw, vh, preferred_element_type=F32)
        y = y * lax.rsqrt(jnp.mean(y * y, axis=-1, keepdims=True) + NORM_EPS)
        y_ref[:, h * RET_DV:(h + 1) * RET_DV] = y
    so_ref[0] = s_ref[...]


def _ret_core(q, kt, v, tabs, s0, *, lc):
    b, dkh, t_len = kt.shape
    nc = t_len // lc
    e = D_INNER
    rowc = lambda bi, ci: (bi * nc + ci, 0)
    state = pl.BlockSpec((1, RET_HEADS, RET_DK, RET_DV), lambda bi, ci: (bi, 0, 0, 0))
    half = RET_DK // 2
    return pl.pallas_call(
        _ret_core_kernel,
        grid=(b, nc),
        in_specs=[pl.BlockSpec((lc, dkh), rowc),
                  pl.BlockSpec((1, dkh, lc), lambda bi, ci: (bi, 0, ci)),
                  pl.BlockSpec((lc, e), rowc),
                  pl.BlockSpec((lc, half), lambda bi, ci: (ci, 0)),
                  pl.BlockSpec((lc, half), lambda bi, ci: (ci, 0)),
                  pl.BlockSpec((half, lc), lambda bi, ci: (0, ci)),
                  pl.BlockSpec((half, lc), lambda bi, ci: (0, ci)),
                  _const((RET_HEADS, lc, lc)), _const((RET_HEADS, lc, 1)),
                  _const((RET_HEADS, 1, lc)), _const((RET_HEADS, 1, 1)), state],
        out_specs=[pl.BlockSpec((lc, e), rowc), state],
        out_shape=[jax.ShapeDtypeStruct((b * t_len, e), F32),
                   jax.ShapeDtypeStruct((b, RET_HEADS, RET_DK, RET_DV), F32)],
        scratch_shapes=[pltpu.VMEM((RET_HEADS, RET_DK, RET_DV), F32)],
        compiler_params=_params("parallel", "arbitrary"),
        name="ret_core",
    )(q, kt, v, tabs["cos"], tabs["sin"], tabs["cos_t"], tabs["sin_t"], tabs["mask"],
      tabs["qdec"], tabs["kdec"], tabs["gl"], s0)


def _ret_tables(pos, lc):
    half = RET_DK // 2
    inv = ROPE_BASE ** (-jnp.arange(half, dtype=F32) / half)
    ang = pos.astype(F32)[:, None] * inv[None, :]
    cos = jnp.cos(ang)
    sin = jnp.sin(ang)
    log_g = jnp.log1p(-jnp.exp2(-5.0 - jnp.arange(RET_HEADS, dtype=F32)))
    n = jnp.arange(lc, dtype=F32)
    diff = n[:, None] - n[None, :]
    mask = jnp.where(diff[None] >= 0, jnp.exp(diff[None] * log_g[:, None, None]), 0.0)
    qdec = jnp.exp((n[None, :] + 1.0) * log_g[:, None])[:, :, None]
    kdec = jnp.exp((lc - 1.0 - n)[None, :] * log_g[:, None])[:, None, :]
    gl = jnp.exp(lc * log_g)[:, None, None]
    return dict(cos=cos, sin=sin, cos_t=cos.T, sin_t=sin.T, mask=mask, qdec=qdec, kdec=kdec, gl=gl)


def _s5_block_weights(bb_re, bb_im, c_re, c_im):
    gl = S5_GROUPS // S5_KB
    eye = jnp.eye(gl, dtype=F32)

    def bdiag_b(bb):
        x = bb.reshape(S5_KB, gl, S5_GROUP, S5_STATE)
        x = x[:, :, :, None, :] * eye[None, :, None, :, None]
        return x.reshape(S5_KB, gl * S5_GROUP, gl * S5_STATE)

    def bdiag_c(c):
        x = jnp.swapaxes(c, 1, 2).reshape(S5_KB, gl, S5_STATE, S5_GROUP)
        x = x[:, :, :, None, :] * eye[None, :, None, :, None]
        return x.reshape(S5_KB, gl * S5_STATE, gl * S5_GROUP)

    bw = jnp.concatenate([bdiag_b(bb_re), bdiag_b(bb_im)], axis=2).astype(BF16)
    cw = jnp.concatenate([bdiag_c(c_re), -bdiag_c(c_im)], axis=1).astype(BF16)
    return bw, cw


def _head_minor(x):
    lead = x.shape[:-1]
    x = x.reshape(lead + (RWKV_HEADS, RWKV_HEAD))
    return jnp.swapaxes(x, -1, -2).reshape(lead + (D_INNER,))


def _key_param(p):
    return jnp.broadcast_to(p.reshape(1, D_INNER), (RWKV_BG, D_INNER))


def _value_param(p):
    return jnp.broadcast_to(p.reshape(RWKV_VGRP, 1, LANES), (RWKV_VGRP, RWKV_BG, LANES))


def _wkv_to_blocks(s, nb):
    x = s.reshape(nb // RWKV_BG, RWKV_BG, RWKV_HEADS, RWKV_VGRP, RWKV_VSUB, RWKV_HEAD)
    x = jnp.transpose(x, (0, 5, 3, 1, 4, 2))
    return x.reshape(nb // RWKV_BG, RWKV_HEAD, RWKV_VGRP, RWKV_BG, LANES)


def _wkv_from_blocks(x, nb):
    x = x.reshape(nb // RWKV_BG, RWKV_HEAD, RWKV_VGRP, RWKV_BG, RWKV_VSUB, RWKV_HEADS)
    x = jnp.transpose(x, (0, 3, 5, 2, 4, 1))
    return x.reshape(nb, RWKV_HEADS, RWKV_HEAD, RWKV_HEAD)


def _run_stream(x, states, wts, norm_pre, norm_post, *, nb, t_len, pos0, tiles):
    d = D_MODEL
    e = D_INNER
    new = {}
    gp = lambda i: (norm_pre[i].reshape(1, d), norm_post[i].reshape(1, d))

    gpre, gpost = gp(0)
    cs = jnp.transpose(states["lru_conv"], (1, 0, 2)).reshape((LRU_CONV_W - 1) * nb, e)
    x, cso, ho = _lru_layer(x, cs, states["lru_h"], gpre, gpost, wts["lru"], nb=nb, tt=tiles["lru_tt"])
    new["lru_conv"] = jnp.transpose(cso.reshape(LRU_CONV_W - 1, nb, e), (1, 0, 2))
    new["lru_h"] = ho

    gpre, gpost = gp(1)
    w = wts["s5"]
    gn = S5_GROUPS * S5_STATE
    uz = _norm_proj(x, gpre, w["w_in"], rows=tiles["rows"])
    y, sre, sim = _s5_core(uz, states["s5_re"].reshape(nb, gn), states["s5_im"].reshape(nb, gn),
                           w["bw"], w["cw"], w["ab_re"], w["ab_im"], w["d"], nb=nb, tt=tiles["s5_tt"])
    x = _s5_out(y, uz, x, w["glu_w"], w["glu_b"], w["w_out"], gpost, rows=tiles["rows"])
    new["s5_re"] = sre.reshape(nb, S5_GROUPS, S5_STATE)
    new["s5_im"] = sim.reshape(nb, S5_GROUPS, S5_STATE)

    gpre, gpost = gp(2)
    w = wts["rwkv"]
    r, k, v, g, dec, a, shift = _rwkv_proj(x, states["rwkv_shift"], gpre, w, nb=nb, tt=tiles["rwkv_tt"])
    as3 = lambda z: z.reshape(t_len, nb, e)
    pk = dict(k_k=_key_param(w["k_k"]), k_a=_key_param(w["k_a"]), r_k=_key_param(w["r_k"]),
              ln_w=_key_param(w["ln_w"]), ln_b=_key_param(w["ln_b"]))
    o, s_new = _rwkv_rec(as3(r), as3(dec), as3(k), as3(v), as3(a), pk,
                         _wkv_to_blocks(states["rwkv_wkv"], nb), tt=tiles["rec_tt"])
    x = _gated_out(o.reshape(t_len * nb, e), g, x, w["w_o"], gpost, rows=tiles["rows"])
    new["rwkv_shift"] = shift
    new["rwkv_wkv"] = _wkv_from_blocks(s_new, nb)

    gpre, gpost = gp(3)
    w = wts["ret"]
    lc = tiles["ret_lc"]
    t_pad = -(-t_len // lc) * lc
    pad = t_pad - t_len
    xb = jnp.transpose(x.reshape(t_len, nb, d), (1, 0, 2))
    xb = jnp.pad(xb, ((0, 0), (pad, 0), (0, 0))).reshape(nb * t_pad, d)
    q, kk, v, g = _ret_proj(xb, gpre, w, rows=tiles["ret_rows"])
    kt = jnp.transpose(kk.reshape(nb, t_pad, RET_HEADS * RET_DK), (0, 2, 1))
    tabs = _ret_tables(pos0 - pad + jnp.arange(t_pad, dtype=jnp.int32), lc)
    y, s_new = _ret_core(q, kt, v, tabs, states["ret"], lc=lc)
    xb = _gated_out(y, g, xb, w["w_o"], gpost, rows=tiles["ret_rows"])
    new["ret"] = s_new
    xb = xb.reshape(nb, t_pad, d)[:, pad:]
    return xb, new


def kernel(x_prompt, x_sample, state_lru_conv, state_lru_h, state_s5_re, state_s5_im, state_rwkv_shift, state_rwkv_wkv, state_ret, meta_tokens, norm_pre, norm_post, lru_w_in, lru_conv_w, lru_conv_b, lru_wa, lru_ba, lru_wx, lru_bx, lru_lam, lru_w_out, s5_w_in, s5_log_dt, s5_a_re, s5_a_im, s5_b_re, s5_b_im, s5_c_re, s5_c_im, s5_d, s5_glu_w, s5_glu_b, s5_w_out, rwkv_mu, rwkv_w_r, rwkv_w_k, rwkv_w_v, rwkv_w_g, rwkv_w0, rwkv_w1, rwkv_w2, rwkv_a0, rwkv_a1, rwkv_a2, rwkv_k_k, rwkv_k_a, rwkv_r_k, rwkv_ln_w, rwkv_ln_b, rwkv_w_o, ret_w_q, ret_w_k, ret_w_v, ret_w_g, ret_w_o):
    d = D_MODEL
    e = D_INNER
    bp, seq, _ = x_prompt.shape
    bs, dec_seq, _ = x_sample.shape
    t_p = seq + N_META
    bf = lambda x: x.astype(BF16)
    vec = lambda x: x.reshape(1, -1)
    hm = _head_minor

    ab_re, ab_im, bb_re, bb_im = _s5_discretize(
        s5_log_dt[0], s5_a_re[0], s5_a_im[0],
        jnp.swapaxes(s5_b_re[0], 1, 2), jnp.swapaxes(s5_b_im[0], 1, 2))
    bw, cw = _s5_block_weights(bb_re, bb_im, s5_c_re[0], s5_c_im[0])
    wts = dict(
        lru=dict(w_in=bf(lru_w_in[0]), conv_w=lru_conv_w[0], conv_b=vec(lru_conv_b[0]),
                 wa=bf(lru_wa[0]), ba=vec(lru_ba[0]), wx=bf(lru_wx[0]), bx=vec(lru_bx[0]),
                 lam=vec(lru_lam[0]), w_out=bf(lru_w_out[0])),
        s5=dict(w_in=bf(s5_w_in[0]), bw=bw, cw=cw, ab_re=vec(ab_re), ab_im=vec(ab_im),
                d=vec(s5_d[0]), glu_w=bf(s5_glu_w[0]), glu_b=vec(s5_glu_b[0]), w_out=bf(s5_w_out[0])),
        rwkv=dict(mu=rwkv_mu[0], w_r=bf(hm(rwkv_w_r[0])), w_k=bf(hm(rwkv_w_k[0])),
                  w_v=bf(hm(rwkv_w_v[0])), w_g=bf(hm(rwkv_w_g[0])), w0=vec(hm(rwkv_w0[0])),
                  w1=bf(rwkv_w1[0]), w2=bf(hm(rwkv_w2[0])), a0=vec(hm(rwkv_a0[0])),
                  a1=bf(rwkv_a1[0]), a2=bf(hm(rwkv_a2[0])),
                  k_k=hm(rwkv_k_k[0]), k_a=hm(rwkv_k_a[0]), r_k=hm(rwkv_r_k[0]),
                  ln_w=hm(rwkv_ln_w[0]), ln_b=hm(rwkv_ln_b[0]), w_o=bf(hm(rwkv_w_o[0].T).T)),
        ret=dict(w_q=bf(ret_w_q[0]), w_k=bf(ret_w_k[0]), w_v=bf(ret_w_v[0]), w_g=bf(ret_w_g[0]),
                 w_o=bf(ret_w_o[0])),
    )

    meta = jnp.broadcast_to(meta_tokens[None].astype(x_prompt.dtype), (bp, N_META, d))
    hp = jnp.concatenate([meta, x_prompt], axis=1)
    xp = jnp.transpose(hp, (1, 0, 2)).reshape(t_p * bp, d)
    z = lambda *s: jnp.zeros(s, F32)
    p_states = dict(lru_conv=z(bp, LRU_CONV_W - 1, e), lru_h=z(bp, e),
                    s5_re=z(bp, S5_GROUPS, S5_STATE), s5_im=z(bp, S5_GROUPS, S5_STATE),
                    rwkv_shift=z(bp, d), rwkv_wkv=z(bp, RWKV_HEADS, RWKV_HEAD, RWKV_HEAD),
                    ret=z(bp, RET_HEADS, RET_DK, RET_DV))
    p_tiles = dict(lru_tt=_divisor_tile(t_p, 48, 1), s5_tt=_divisor_tile(t_p, 16, 1),
                   rwkv_tt=_divisor_tile(t_p, 16, 1), rec_tt=_divisor_tile(t_p, 8, 1),
                   rows=_divisor_tile(t_p * bp, 384, 8), ret_lc=RET_CHUNK,
                   ret_rows=_divisor_tile(bp * (-(-t_p // RET_CHUNK) * RET_CHUNK), 512, 8))
    yp, new_p = _run_stream(xp, p_states, wts, norm_pre, norm_post, nb=bp, t_len=t_p, pos0=0,
                            tiles=p_tiles)

    xs = jnp.transpose(x_sample, (1, 0, 2)).reshape(dec_seq * bs, d)
    s_states = dict(lru_conv=state_lru_conv[0], lru_h=state_lru_h[0], s5_re=state_s5_re[0],
                    s5_im=state_s5_im[0], rwkv_shift=state_rwkv_shift[0],
                    rwkv_wkv=state_rwkv_wkv[0], ret=state_ret[0])
    s_rows = _divisor_tile(dec_seq * bs, 256, 8)
    s_tiles = dict(lru_tt=2, s5_tt=1, rwkv_tt=2, rec_tt=dec_seq, rows=s_rows, ret_lc=dec_seq,
                   ret_rows=s_rows)
    ys, new_s = _run_stream(xs, s_states, wts, norm_pre, norm_post, nb=bs, t_len=dec_seq,
                            pos0=PAST_LEN, tiles=s_tiles)

    names = ("lru_conv", "lru_h", "s5_re", "s5_im", "rwkv_shift", "rwkv_wkv", "ret")
    return ((yp[:, N_META:], ys)
            + tuple(new_p[n][None] for n in names)
            + tuple(new_s[n][None] for n in names))
```

```python
import functools
import math

import jax
import jax.numpy as jnp
from jax import lax
from jax.experimental import pallas as pl
from jax.experimental.pallas import tpu as pltpu

F32 = jnp.float32
BF16 = jnp.bfloat16

D_MODEL = 1024
D_INNER = 2048
N_META = 16
NORM_EPS = 1e-6
LRU_CONV_W = 4
LRU_BLOCKS = 16
LRU_BLOCK = 128
LRU_C = 8.0
S5_GROUP = 16
S5_GROUPS = 128
S5_STATE = 64
S5_KB = 8
RWKV_HEAD = 64
RWKV_HEADS = 32
RWKV_LN_EPS = 64e-5
RET_HEADS = 4
RET_DK = 256
RET_DV = 512
ROPE_BASE = 10000.0
PAST_LEN = 16384
RET_CHUNK = 128

LANES = 128
VMEM_LIMIT = 56 * 1024 * 1024


def _params(*sem):
    return pltpu.CompilerParams(dimension_semantics=sem, vmem_limit_bytes=VMEM_LIMIT)


def _const(shape):
    zeros = (0,) * len(shape)
    return pl.BlockSpec(shape, lambda *_: zeros, pipeline_mode=pl.Buffered(1))


def _const_out(shape):
    zeros = (0,) * len(shape)
    return pl.BlockSpec(shape, lambda *_: zeros)


def _divisor_tile(n, pref, mult):
    best = mult
    t = mult
    while t <= min(n, pref):
        if n % t == 0:
            best = t
        t += mult
    assert n % best == 0
    return best


def _rms(x, g):
    return x * lax.rsqrt(jnp.mean(x * x, axis=-1, keepdims=True) + NORM_EPS) * g


def _mm(a, w):
    return jnp.dot(a.astype(BF16), w, preferred_element_type=F32)


def _silu(x):
    return x * jax.nn.sigmoid(x)


def _softplus(x):
    return jnp.maximum(x, 0.0) + jnp.log1p(jnp.exp(-jnp.abs(x)))


def _lru_kernel(x_ref, cs_ref, h0_ref, gpre_ref, gpost_ref, win_ref, cw_ref, cb_ref,
                wa_ref, ba_ref, wx_ref, bx_ref, lam_ref, wout_ref,
                xo_ref, cso_ref, ho_ref,
                uz_ref, b_ref, tail_ref, h_ref, *, nb, tt):
    rows = nb * tt
    e = D_INNER

    @pl.when(pl.program_id(0) == 0)
    def _():
        tail_ref[...] = cs_ref[...]
        h_ref[...] = h0_ref[...]

    xn = _rms(x_ref[...], gpre_ref[...])
    uz_ref[...] = _mm(xn, win_ref[...])

    for j in range(LRU_BLOCKS):
        sl = slice(j * LRU_BLOCK, (j + 1) * LRU_BLOCK)
        ext = jnp.concatenate([tail_ref[:, sl], uz_ref[:, sl]], axis=0)
        cw = cw_ref[:, sl]
        xc = cb_ref[:, sl]
        for jj in range(LRU_CONV_W):
            xc = xc + ext[jj * nb:jj * nb + rows] * cw[jj:jj + 1]
        tail_ref[:, sl] = ext[rows:rows + (LRU_CONV_W - 1) * nb]
        gate_r = jax.nn.sigmoid(_mm(xc, wa_ref[j]) + ba_ref[:, sl])
        gate_i = jax.nn.sigmoid(_mm(xc, wx_ref[j]) + bx_ref[:, sl])
        log_a = -LRU_C * gate_r * _softplus(-lam_ref[:, sl])
        a = jnp.exp(log_a)
        bx = jnp.sqrt(-jnp.tanh(log_a) * (a * a + 1.0)) * gate_i * xc
        h = h_ref[:, sl]
        hs = []
        for t in range(tt):
            h = a[t * nb:(t + 1) * nb] * h + bx[t * nb:(t + 1) * nb]
            hs.append(h)
        h_ref[:, sl] = h
        b_ref[:, sl] = jnp.concatenate(hs, axis=0)

    y = b_ref[...] * _silu(uz_ref[:, e:])
    out = _mm(y, wout_ref[...])
    xo_ref[...] = x_ref[...] + _rms(out, gpost_ref[...])
    cso_ref[...] = tail_ref[...]
    ho_ref[...] = h_ref[...]


def _lru_layer(x, conv_state, h0, gpre, gpost, w, *, nb, tt):
    r, d = x.shape
    e = D_INNER
    rows = nb * tt
    ctail = (LRU_CONV_W - 1) * nb
    row = lambda i: (i, 0)
    return pl.pallas_call(
        functools.partial(_lru_kernel, nb=nb, tt=tt),
        grid=(r // rows,),
        in_specs=[
            pl.BlockSpec((rows, d), row), _const((ctail, e)), _const((nb, e)),
            _const((1, d)), _const((1, d)), _const((d, 2 * e)),
            _const((LRU_CONV_W, e)), _const((1, e)),
            _const((LRU_BLOCKS, LRU_BLOCK, LRU_BLOCK)), _const((1, e)),
            _const((LRU_BLOCKS, LRU_BLOCK, LRU_BLOCK)), _const((1, e)),
            _const((1, e)), _const((e, d)),
        ],
        out_specs=[pl.BlockSpec((rows, d), row), _const_out((ctail, e)), _const_out((nb, e))],
        out_shape=[jax.ShapeDtypeStruct((r, d), F32),
                   jax.ShapeDtypeStruct((ctail, e), F32),
                   jax.ShapeDtypeStruct((nb, e), F32)],
        scratch_shapes=[pltpu.VMEM((rows, 2 * e), F32), pltpu.VMEM((rows, e), F32),
                        pltpu.VMEM((ctail, e), F32),
                        pltpu.VMEM((nb, e), F32)],
        compiler_params=_params("arbitrary"),
        name="lru_layer",
    )(x, conv_state, h0, gpre, gpost, w["w_in"], w["conv_w"], w["conv_b"],
      w["wa"], w["ba"], w["wx"], w["bx"], w["lam"], w["w_out"])


def _s5_disc_kernel(logdt_ref, are_ref, aim_ref, bre_ref, bim_ref,
                    abre_ref, abim_ref, bbre_ref, bbim_ref):
    dt = jnp.exp(logdt_ref[...])
    a_re = are_ref[...]
    a_im = aim_ref[...]
    mag = jnp.exp(dt * a_re)
    ang = dt * a_im
    ab_re = mag * jnp.cos(ang)
    ab_im = mag * jnp.sin(ang)
    den = a_re * a_re + a_im * a_im
    f_re = ((ab_re - 1.0) * a_re + ab_im * a_im) / den
    f_im = (ab_im * a_re - (ab_re - 1.0) * a_im) / den
    abre_ref[...] = ab_re
    abim_ref[...] = ab_im
    b_re = bre_ref[...]
    b_im = bim_ref[...]
    fr = f_re[:, None, :]
    fi = f_im[:, None, :]
    bbre_ref[...] = fr * b_re - fi * b_im
    bbim_ref[...] = fr * b_im + fi * b_re


def _s5_discretize(log_dt, a_re, a_im, b_re, b_im):
    g, n = a_re.shape
    c = b_re.shape[1]
    return pl.pallas_call(
        _s5_disc_kernel,
        out_shape=[jax.ShapeDtypeStruct((g, n), F32), jax.ShapeDtypeStruct((g, n), F32),
                   jax.ShapeDtypeStruct((g, c, n), F32), jax.ShapeDtypeStruct((g, c, n), F32)],
        name="s5_discretize",
    )(log_dt.reshape(g, 1), a_re, a_im, b_re, b_im)


def _gelu_tanh(x):
    return 0.5 * x * (1.0 + jnp.tanh(math.sqrt(2.0 / math.pi) * (x + 0.044715 * (x * x * x))))


def _s5_core_kernel(x_ref, gpre_ref, wu_ref, sre_ref, sim_ref, bw_ref, cw_ref, abre_ref, abim_ref,
                    d_ref, y_ref, sreo_ref, simo_ref,
                    xn_ref, bu_ref, xr_ref, xi_ref, *, nb, tt):
    gn = S5_GROUPS * S5_STATE // S5_KB
    ch = D_INNER // S5_KB

    @pl.when(pl.program_id(0) == 0)
    def _():
        xr_ref[...] = sre_ref[...]
        xi_ref[...] = sim_ref[...]

    xn_ref[...] = _rms(x_ref[...], gpre_ref[...]).astype(BF16)

    for kb in range(S5_KB):
        lanes = slice(kb * gn, (kb + 1) * gn)
        cols = slice(kb * ch, (kb + 1) * ch)
        u = jnp.dot(xn_ref[...], wu_ref[:, cols], preferred_element_type=F32)
        bu_ref[...] = _mm(u, bw_ref[kb])
        ar = abre_ref[:, lanes]
        ai = abim_ref[:, lanes]

        def step(t, carry):
            xr, xi = carry
            r = pl.ds(pl.multiple_of(t * nb, nb), nb)
            nr = ar * xr - ai * xi + bu_ref[r, :gn]
            ni = ar * xi + ai * xr + bu_ref[r, gn:]
            bu_ref[r, :gn] = nr
            bu_ref[r, gn:] = ni
            return nr, ni

        xr, xi = lax.fori_loop(0, tt, step, (xr_ref[:, lanes], xi_ref[:, lanes]), unroll=True)
        xr_ref[:, lanes] = xr
        xi_ref[:, lanes] = xi
        y = _mm(bu_ref[...], cw_ref[kb]) + d_ref[:, cols] * u
        y_ref[:, cols] = _gelu_tanh(y)

    sreo_ref[...] = xr_ref[...]
    simo_ref[...] = xi_ref[...]


def _s5_core(x, gpre, w, s_re, s_im, *, nb, tt):
    r, d = x.shape
    e = D_INNER
    gn = S5_GROUPS * S5_STATE
    rows = nb * tt
    row = lambda i: (i, 0)
    return pl.pallas_call(
        functools.partial(_s5_core_kernel, nb=nb, tt=tt),
        grid=(r // rows,),
        in_specs=[pl.BlockSpec((rows, d), row), _const((1, d)), _const((d, e)),
                  _const((nb, gn)), _const((nb, gn)), _const(w["bw"].shape), _const(w["cw"].shape),
                  _const((1, gn)), _const((1, gn)), _const((1, e))],
        out_specs=[pl.BlockSpec((rows, e), row), _const_out((nb, gn)), _const_out((nb, gn))],
        out_shape=[jax.ShapeDtypeStruct((r, e), F32), jax.ShapeDtypeStruct((nb, gn), F32),
                   jax.ShapeDtypeStruct((nb, gn), F32)],
        scratch_shapes=[pltpu.VMEM((rows, d), BF16), pltpu.VMEM((rows, 2 * gn // S5_KB), F32),
                        pltpu.VMEM((nb, gn), F32), pltpu.VMEM((nb, gn), F32)],
        compiler_params=_params("arbitrary"),
        name="s5_core",
    )(x, gpre, w["w_u"], s_re, s_im, w["bw"], w["cw"], w["ab_re"], w["ab_im"], w["d"])


def _s5_out_kernel(y_ref, x_ref, gpre_ref, wg_ref, gluw_ref, glub_ref, wout_ref, gpost_ref, xo_ref):
    x = x_ref[...]
    gate = _mm(_rms(x, gpre_ref[...]), wg_ref[...])
    y = y_ref[...]
    y = y * jax.nn.sigmoid(_mm(y, gluw_ref[...]) + glub_ref[...])
    y = y * _silu(gate)
    xo_ref[...] = x + _rms(_mm(y, wout_ref[...]), gpost_ref[...])


def _s5_out(y, x, gpre, gpost, w, *, rows):
    r, d = x.shape
    e = D_INNER
    row = lambda i: (i, 0)
    return pl.pallas_call(
        _s5_out_kernel,
        grid=(r // rows,),
        in_specs=[pl.BlockSpec((rows, e), row), pl.BlockSpec((rows, d), row), _const((1, d)),
                  _const((d, e)), _const((e, e)), _const((1, e)), _const((e, d)), _const((1, d))],
        out_specs=pl.BlockSpec((rows, d), row),
        out_shape=jax.ShapeDtypeStruct((r, d), F32),
        compiler_params=_params("parallel"),
        name="s5_out",
    )(y, x, gpre, w["w_gate"], w["glu_w"], w["glu_b"], w["w_out"], gpost)


def _rwkv_proj_kernel(x_ref, xprev_ref, gpre_ref, mu_ref, wr_ref, wk_ref, wv_ref, wg_ref,
                      w0_ref, w1_ref, w2_ref, a0_ref, a1_ref, a2_ref,
                      r_ref, k_ref, v_ref, g_ref, dec_ref, a_ref, shift_ref,
                      prev_ref, *, nb, tt):
    rows = nb * tt

    @pl.when(pl.program_id(0) == 0)
    def _():
        prev_ref[...] = xprev_ref[...]

    xn = _rms(x_ref[...], gpre_ref[...])
    if tt > 1:
        shifted = jnp.concatenate([prev_ref[...], xn[:rows - nb]], axis=0)
    else:
        shifted = prev_ref[...]
    prev_ref[...] = xn[rows - nb:]
    shift_ref[...] = xn[rows - nb:]
    xx = shifted - xn
    mix = lambda n: xn + xx * mu_ref[n:n + 1, :]
    r_ref[...] = _mm(mix(0), wr_ref[...])
    k_ref[...] = _mm(mix(2), wk_ref[...])
    v_ref[...] = _mm(mix(3), wv_ref[...])
    g_ref[...] = _silu(_mm(mix(5), wg_ref[...]))
    w_raw = w0_ref[...] + _mm(jnp.tanh(_mm(mix(1), w1_ref[...])), w2_ref[...])
    dec_ref[...] = jnp.exp(-(jax.nn.sigmoid(w_raw) * math.exp(-0.5)))
    a_ref[...] = jax.nn.sigmoid(a0_ref[...] + _mm(_mm(mix(4), a1_ref[...]), a2_ref[...]))


def _rwkv_proj(x, x_prev, gpre, w, *, nb, tt):
    r, d = x.shape
    e = D_INNER
    rows = nb * tt
    lora = w["w1"].shape[1]
    row = lambda i: (i, 0)
    big = pl.BlockSpec((rows, e), row)
    return pl.pallas_call(
        functools.partial(_rwkv_proj_kernel, nb=nb, tt=tt),
        grid=(r // rows,),
        in_specs=[pl.BlockSpec((rows, d), row), _const((nb, d)), _const((1, d)), _const((6, d)),
                  _const((d, e)), _const((d, e)), _const((d, e)), _const((d, e)),
                  _const((1, e)), _const((d, lora)), _const((lora, e)),
                  _const((1, e)), _const((d, lora)), _const((lora, e))],
        out_specs=[big, big, big, big, big, big, _const_out((nb, d))],
        out_shape=[jax.ShapeDtypeStruct((r, e), F32)] * 6 + [jax.ShapeDtypeStruct((nb, d), F32)],
        scratch_shapes=[pltpu.VMEM((nb, d), F32)],
        compiler_params=_params("arbitrary"),
        name="rwkv_proj",
    )(x, x_prev, gpre, w["mu"], w["w_r"], w["w_k"], w["w_v"], w["w_g"],
      w["w0"], w["w1"], w["w2"], w["a0"], w["a1"], w["a2"])


RWKV_VSUB = 4
RWKV_VGRP = RWKV_HEAD // RWKV_VSUB
RWKV_BG = 8
RWKV_VBLK = 8
RWKV_UNROLL = 8


def _seg_sum(x):
    lane_axis = x.ndim - 1
    x = x + pltpu.roll(x, 2 * RWKV_HEADS, lane_axis)
    return x + pltpu.roll(x, RWKV_HEADS, lane_axis)


def _rwkv_rec_kernel(r_ref, w_ref, k_ref, v_ref, a_ref, kk_ref, ka_ref, rk_ref, lnw_ref, lnb_ref,
                     s0_ref, o_ref, so_ref, s_ref, vec_ref, sa_scale_ref, *, tt):
    n = RWKV_HEAD
    nh = RWKV_HEADS
    shape = (RWKV_BG, LANES)

    @pl.when(pl.program_id(1) == 0)
    def _():
        s_ref[...] = s0_ref[0]

    tiles = [slice(j * LANES, (j + 1) * LANES) for j in range(RWKV_VGRP)]
    seg = lax.broadcasted_iota(jnp.int32, (tt,) + shape, 2) // nh
    seg_low = seg < 2
    seg_odd = (seg & 1) == 1

    def rep4(x):
        rolled = [x] + [pltpu.roll(x, q * nh, 2) for q in range(1, RWKV_VSUB)]
        pair = [jnp.where(seg_low, rolled[i], rolled[(i + 2) % RWKV_VSUB]) for i in range(RWKV_VSUB)]
        return [jnp.where(seg_odd, pair[(1 - s_) % RWKV_VSUB], pair[(-s_) % RWKV_VSUB])
                for s_ in range(RWKV_VSUB)]

    n2 = jnp.zeros((tt,) + shape, F32)
    bon = jnp.zeros((tt,) + shape, F32)
    for j, sl in enumerate(tiles):
        r = r_ref[:, :, sl]
        k = k_ref[:, :, sl]
        a = a_ref[:, :, sl]
        kk = k * kk_ref[:, sl]
        k2 = k * (1.0 + (a - 1.0) * ka_ref[:, sl])
        n2 = n2 + kk * kk
        bon = bon + r * k2 * rk_ref[:, sl]
        for idx, x in enumerate((r, w_ref[:, :, sl], kk, kk * a, k2)):
            for s_, rep in enumerate(rep4(x)):
                vec_ref[:, idx, RWKV_VSUB * j + s_] = rep
    inv = lax.rsqrt(jnp.maximum(_seg_sum(n2), 1e-24))
    sa_scale_ref[...] = -(inv * inv)
    bon = _seg_sum(bon)

    def token(t, carry):
        sa_scale = sa_scale_ref[t]
        zero = jnp.zeros((RWKV_VBLK,) + shape, F32)
        for vb in range(RWKV_VGRP // RWKV_VBLK):
            vsl = slice(vb * RWKV_VBLK, (vb + 1) * RWKV_VBLK)
            vv = jnp.stack([v_ref[t, :, sl] for sl in tiles[vsl]])

            def sa_step(c, acc):
                return acc + s_ref[c, vsl] * vec_ref[t, 2, c]

            sa = lax.fori_loop(0, n, sa_step, zero, unroll=RWKV_UNROLL) * sa_scale

            def update_step(c, acc):
                s_new = s_ref[c, vsl] * vec_ref[t, 1, c] + sa * vec_ref[t, 3, c] + vv * vec_ref[t, 4, c]
                s_ref[c, vsl] = s_new
                return acc + s_new * vec_ref[t, 0, c]

            y = lax.fori_loop(0, n, update_step, zero, unroll=RWKV_UNROLL)
            for i, sl in enumerate(tiles[vsl]):
                o_ref[t, :, sl] = y[i]
        return carry

    lax.fori_loop(0, tt, token, 0)

    tot = o_ref[:, :, tiles[0]]
    for sl in tiles[1:]:
        tot = tot + o_ref[:, :, sl]
    mean = _seg_sum(tot) * (1.0 / n)
    sq = jnp.zeros((tt,) + shape, F32)
    for sl in tiles:
        yc = o_ref[:, :, sl] - mean
        sq = sq + yc * yc
    rstd = lax.rsqrt(_seg_sum(sq) * (1.0 / n) + RWKV_LN_EPS)
    for sl in tiles:
        o_ref[:, :, sl] = ((o_ref[:, :, sl] - mean) * rstd * lnw_ref[:, sl] + lnb_ref[:, sl]
                           + bon * v_ref[:, :, sl])

    @pl.when(pl.program_id(1) == pl.num_programs(1) - 1)
    def _():
        so_ref[0] = s_ref[...]


def _rwkv_rec(r, w, k, v, a, pk, s0, *, tt):
    t_len, nb, e = r.shape
    n = RWKV_HEAD
    tok = pl.BlockSpec((tt, RWKV_BG, e), lambda bi, ti: (ti, bi, 0))
    st = pl.BlockSpec((1, n, RWKV_VGRP, RWKV_BG, LANES), lambda bi, ti: (bi, 0, 0, 0, 0))
    return pl.pallas_call(
        functools.partial(_rwkv_rec_kernel, tt=tt),
        grid=(nb // RWKV_BG, t_len // tt),
        in_specs=[tok] * 5 + [_const((RWKV_BG, e))] * 5 + [st],
        out_specs=[tok, st],
        out_shape=[jax.ShapeDtypeStruct((t_len, nb, e), F32), jax.ShapeDtypeStruct(s0.shape, F32)],
        scratch_shapes=[pltpu.VMEM((n, RWKV_VGRP, RWKV_BG, LANES), F32),
                        pltpu.VMEM((tt, 5, n, RWKV_BG, LANES), F32),
                        pltpu.VMEM((tt, RWKV_BG, LANES), F32)],
        compiler_params=_params("parallel", "arbitrary"),
        name="rwkv_recurrence",
    )(r, w, k, v, a, pk["k_k"], pk["k_a"], pk["r_k"], pk["ln_w"], pk["ln_b"], s0)


def _gated_out_kernel(y_ref, g_ref, x_ref, w_ref, gpost_ref, xo_ref):
    y = y_ref[...] * g_ref[...]
    xo_ref[...] = x_ref[...] + _rms(_mm(y, w_ref[...]), gpost_ref[...])


def _gated_out(y, g, x, w, gpost, *, rows):
    r, d = x.shape
    e = D_INNER
    row = lambda i: (i, 0)
    return pl.pallas_call(
        _gated_out_kernel,
        grid=(r // rows,),
        in_specs=[pl.BlockSpec((rows, e), row), pl.BlockSpec((rows, e), row),
                  pl.BlockSpec((rows, d), row), _const((e, d)), _const((1, d))],
        out_specs=pl.BlockSpec((rows, d), row),
        out_shape=jax.ShapeDtypeStruct((r, d), F32),
        compiler_params=_params("parallel"),
        name="gated_out",
    )(y, g, x, w, gpost)


def _ret_proj_kernel(x_ref, gpre_ref, wq_ref, wk_ref, wv_ref, wg_ref, q_ref, k_ref, v_ref, g_ref):
    xn = _rms(x_ref[...], gpre_ref[...]).astype(BF16)
    q_ref[...] = jnp.dot(xn, wq_ref[...], preferred_element_type=F32)
    k_ref[...] = jnp.dot(xn, wk_ref[...], preferred_element_type=F32)
    v_ref[...] = jnp.dot(xn, wv_ref[...], preferred_element_type=F32)
    g_ref[...] = _silu(jnp.dot(xn, wg_ref[...], preferred_element_type=F32))


def _ret_proj(x, gpre, w, *, rows):
    r, d = x.shape
    e = D_INNER
    row = lambda i: (i, 0)
    return pl.pallas_call(
        _ret_proj_kernel,
        grid=(r // rows,),
        in_specs=[pl.BlockSpec((rows, d), row), _const((1, d)), _const((d, d)), _const((d, d)),
                  _const((d, e)), _const((d, e))],
        out_specs=[pl.BlockSpec((rows, d), row), pl.BlockSpec((rows, d), row),
                   pl.BlockSpec((rows, e), row), pl.BlockSpec((rows, e), row)],
        out_shape=[jax.ShapeDtypeStruct((r, d), F32), jax.ShapeDtypeStruct((r, d), F32),
                   jax.ShapeDtypeStruct((r, e), F32), jax.ShapeDtypeStruct((r, e), F32)],
        compiler_params=_params("parallel"),
        name="ret_proj",
    )(x, gpre, w["w_q"], w["w_k"], w["w_v"], w["w_g"])


def _ret_core_kernel(q_ref, kt_ref, v_ref, cos_ref, sin_ref, cost_ref, sint_ref, mask_ref,
                     qdec_ref, kdec_ref, gl_ref, s0_ref, y_ref, so_ref, s_ref):
    half = RET_DK // 2

    @pl.when(pl.program_id(1) == 0)
    def _():
        s_ref[...] = s0_ref[0]

    cos = cos_ref[...]
    sin = sin_ref[...]
    cost = cost_ref[...]
    sint = sint_ref[...]
    for h in range(RET_HEADS):
        q1 = q_ref[:, h * RET_DK:h * RET_DK + half]
        q2 = q_ref[:, h * RET_DK + half:(h + 1) * RET_DK]
        qr = jnp.concatenate([q1 * cos - q2 * sin, q2 * cos + q1 * sin], axis=1)
        k1 = kt_ref[0, h * RET_DK:h * RET_DK + half, :]
        k2 = kt_ref[0, h * RET_DK + half:(h + 1) * RET_DK, :]
        krt = jnp.concatenate([k1 * cost - k2 * sint, k2 * cost + k1 * sint], axis=0)
        krt = krt * (RET_DK ** -0.5)
        vh = v_ref[:, h * RET_DV:(h + 1) * RET_DV].astype(BF16)
        qb = qr.astype(BF16)
        scores = jnp.dot(qb, krt.astype(BF16), preferred_element_type=F32) * mask_ref[h]
        s_old = s_ref[h]
        y = jnp.dot(scores.astype(BF16), vh, preferred_element_type=F32)
        y = y + jnp.dot(qb, s_old.astype(BF16), preferred_element_type=F32) * qdec_ref[h]
        kw = (krt * kdec_ref[h]).astype(BF16)
        s_ref[h] = gl_ref[h] * s_old + jnp.dot(kw, vh, preferred_element_type=F32)
        y = y * lax.rsqrt(jnp.mean(y * y, axis=-1, keepdims=True) + NORM_EPS)
        y_ref[:, h * RET_DV:(h + 1) * RET_DV] = y
    so_ref[0] = s_ref[...]


def _ret_core(q, kt, v, tabs, s0, *, lc):
    b, dkh, t_len = kt.shape
    nc = t_len // lc
    e = D_INNER
    rowc = lambda bi, ci: (bi * nc + ci, 0)
    state = pl.BlockSpec((1, RET_HEADS, RET_DK, RET_DV), lambda bi, ci: (bi, 0, 0, 0))
    half = RET_DK // 2
    return pl.pallas_call(
        _ret_core_kernel,
        grid=(b, nc),
        in_specs=[pl.BlockSpec((lc, dkh), rowc),
                  pl.BlockSpec((1, dkh, lc), lambda bi, ci: (bi, 0, ci)),
                  pl.BlockSpec((lc, e), rowc),
                  pl.BlockSpec((lc, half), lambda bi, ci: (ci, 0)),
                  pl.BlockSpec((lc, half), lambda bi, ci: (ci, 0)),
                  pl.BlockSpec((half, lc), lambda bi, ci: (0, ci)),
                  pl.BlockSpec((half, lc), lambda bi, ci: (0, ci)),
                  _const((RET_HEADS, lc, lc)), _const((RET_HEADS, lc, 1)),
                  _const((RET_HEADS, 1, lc)), _const((RET_HEADS, 1, 1)), state],
        out_specs=[pl.BlockSpec((lc, e), rowc), state],
        out_shape=[jax.ShapeDtypeStruct((b * t_len, e), F32),
                   jax.ShapeDtypeStruct((b, RET_HEADS, RET_DK, RET_DV), F32)],
        scratch_shapes=[pltpu.VMEM((RET_HEADS, RET_DK, RET_DV), F32)],
        compiler_params=_params("parallel", "arbitrary"),
        name="ret_core",
    )(q, kt, v, tabs["cos"], tabs["sin"], tabs["cos_t"], tabs["sin_t"], tabs["mask"],
      tabs["qdec"], tabs["kdec"], tabs["gl"], s0)


def _ret_tables(pos, lc):
    half = RET_DK // 2
    inv = ROPE_BASE ** (-jnp.arange(half, dtype=F32) / half)
    ang = pos.astype(F32)[:, None] * inv[None, :]
    cos = jnp.cos(ang)
    sin = jnp.sin(ang)
    log_g = jnp.log1p(-jnp.exp2(-5.0 - jnp.arange(RET_HEADS, dtype=F32)))
    n = jnp.arange(lc, dtype=F32)
    diff = n[:, None] - n[None, :]
    mask = jnp.where(diff[None] >= 0, jnp.exp(diff[None] * log_g[:, None, None]), 0.0)
    qdec = jnp.exp((n[None, :] + 1.0) * log_g[:, None])[:, :, None]
    kdec = jnp.exp((lc - 1.0 - n)[None, :] * log_g[:, None])[:, None, :]
    gl = jnp.exp(lc * log_g)[:, None, None]
    return dict(cos=cos, sin=sin, cos_t=cos.T, sin_t=sin.T, mask=mask, qdec=qdec, kdec=kdec, gl=gl)


def _s5_block_weights(bb_re, bb_im, c_re, c_im):
    gl = S5_GROUPS // S5_KB
    eye = jnp.eye(gl, dtype=F32)

    def bdiag_b(bb):
        x = bb.reshape(S5_KB, gl, S5_GROUP, S5_STATE)
        x = x[:, :, :, None, :] * eye[None, :, None, :, None]
        return x.reshape(S5_KB, gl * S5_GROUP, gl * S5_STATE)

    def bdiag_c(c):
        x = jnp.swapaxes(c, 1, 2).reshape(S5_KB, gl, S5_STATE, S5_GROUP)
        x = x[:, :, :, None, :] * eye[None, :, None, :, None]
        return x.reshape(S5_KB, gl * S5_STATE, gl * S5_GROUP)

    bw = jnp.concatenate([bdiag_b(bb_re), bdiag_b(bb_im)], axis=2).astype(BF16)
    cw = jnp.concatenate([bdiag_c(c_re), -bdiag_c(c_im)], axis=1).astype(BF16)
    return bw, cw


def _head_minor(x):
    lead = x.shape[:-1]
    x = x.reshape(lead + (RWKV_HEADS, RWKV_HEAD))
    return jnp.swapaxes(x, -1, -2).reshape(lead + (D_INNER,))


def _key_param(p):
    return jnp.broadcast_to(p.reshape(1, D_INNER), (RWKV_BG, D_INNER))


def _value_param(p):
    return jnp.broadcast_to(p.reshape(RWKV_VGRP, 1, LANES), (RWKV_VGRP, RWKV_BG, LANES))


def _wkv_to_blocks(s, nb):
    x = s.reshape(nb // RWKV_BG, RWKV_BG, RWKV_HEADS, RWKV_VGRP, RWKV_VSUB, RWKV_HEAD)
    x = jnp.transpose(x, (0, 5, 3, 1, 4, 2))
    return x.reshape(nb // RWKV_BG, RWKV_HEAD, RWKV_VGRP, RWKV_BG, LANES)


def _wkv_from_blocks(x, nb):
    x = x.reshape(nb // RWKV_BG, RWKV_HEAD, RWKV_VGRP, RWKV_BG, RWKV_VSUB, RWKV_HEADS)
    x = jnp.transpose(x, (0, 3, 5, 2, 4, 1))
    return x.reshape(nb, RWKV_HEADS, RWKV_HEAD, RWKV_HEAD)


def _run_stream(x, states, wts, norm_pre, norm_post, *, nb, t_len, pos0, tiles):
    d = D_MODEL
    e = D_INNER
    new = {}
    gp = lambda i: (norm_pre[i].reshape(1, d), norm_post[i].reshape(1, d))

    gpre, gpost = gp(0)
    cs = jnp.transpose(states["lru_conv"], (1, 0, 2)).reshape((LRU_CONV_W - 1) * nb, e)
    x, cso, ho = _lru_layer(x, cs, states["lru_h"], gpre, gpost, wts["lru"], nb=nb, tt=tiles["lru_tt"])
    new["lru_conv"] = jnp.transpose(cso.reshape(LRU_CONV_W - 1, nb, e), (1, 0, 2))
    new["lru_h"] = ho

    gpre, gpost = gp(1)
    w = wts["s5"]
    gn = S5_GROUPS * S5_STATE
    y, sre, sim = _s5_core(x, gpre, w, states["s5_re"].reshape(nb, gn),
                           states["s5_im"].reshape(nb, gn), nb=nb, tt=tiles["s5_tt"])
    x = _s5_out(y, x, gpre, gpost, w, rows=tiles["rows"])
    new["s5_re"] = sre.reshape(nb, S5_GROUPS, S5_STATE)
    new["s5_im"] = sim.reshape(nb, S5_GROUPS, S5_STATE)

    gpre, gpost = gp(2)
    w = wts["rwkv"]
    r, k, v, g, dec, a, shift = _rwkv_proj(x, states["rwkv_shift"], gpre, w, nb=nb, tt=tiles["rwkv_tt"])
    as3 = lambda z: z.reshape(t_len, nb, e)
    pk = dict(k_k=_key_param(w["k_k"]), k_a=_key_param(w["k_a"]), r_k=_key_param(w["r_k"]),
              ln_w=_key_param(w["ln_w"]), ln_b=_key_param(w["ln_b"]))
    o, s_new = _rwkv_rec(as3(r), as3(dec), as3(k), as3(v), as3(a), pk,
                         _wkv_to_blocks(states["rwkv_wkv"], nb), tt=tiles["rec_tt"])
    x = _gated_out(o.reshape(t_len * nb, e), g, x, w["w_o"], gpost, rows=tiles["rows"])
    new["rwkv_shift"] = shift
    new["rwkv_wkv"] = _wkv_from_blocks(s_new, nb)

    gpre, gpost = gp(3)
    w = wts["ret"]
    lc = tiles["ret_lc"]
    t_pad = -(-t_len // lc) * lc
    pad = t_pad - t_len
    xb = jnp.transpose(x.reshape(t_len, nb, d), (1, 0, 2))
    xb = jnp.pad(xb, ((0, 0), (pad, 0), (0, 0))).reshape(nb * t_pad, d)
    q, kk, v, g = _ret_proj(xb, gpre, w, rows=tiles["ret_rows"])
    kt = jnp.transpose(kk.reshape(nb, t_pad, RET_HEADS * RET_DK), (0, 2, 1))
    tabs = _ret_tables(pos0 - pad + jnp.arange(t_pad, dtype=jnp.int32), lc)
    y, s_new = _ret_core(q, kt, v, tabs, states["ret"], lc=lc)
    xb = _gated_out(y, g, xb, w["w_o"], gpost, rows=tiles["ret_rows"])
    new["ret"] = s_new
    xb = xb.reshape(nb, t_pad, d)[:, pad:]
    return xb, new


def kernel(x_prompt, x_sample, state_lru_conv, state_lru_h, state_s5_re, state_s5_im, state_rwkv_shift, state_rwkv_wkv, state_ret, meta_tokens, norm_pre, norm_post, lru_w_in, lru_conv_w, lru_conv_b, lru_wa, lru_ba, lru_wx, lru_bx, lru_lam, lru_w_out, s5_w_in, s5_log_dt, s5_a_re, s5_a_im, s5_b_re, s5_b_im, s5_c_re, s5_c_im, s5_d, s5_glu_w, s5_glu_b, s5_w_out, rwkv_mu, rwkv_w_r, rwkv_w_k, rwkv_w_v, rwkv_w_g, rwkv_w0, rwkv_w1, rwkv_w2, rwkv_a0, rwkv_a1, rwkv_a2, rwkv_k_k, rwkv_k_a, rwkv_r_k, rwkv_ln_w, rwkv_ln_b, rwkv_w_o, ret_w_q, ret_w_k, ret_w_v, ret_w_g, ret_w_o):
    d = D_MODEL
    e = D_INNER
    bp, seq, _ = x_prompt.shape
    bs, dec_seq, _ = x_sample.shape
    t_p = seq + N_META
    bf = lambda x: x.astype(BF16)
    vec = lambda x: x.reshape(1, -1)
    hm = _head_minor

    ab_re, ab_im, bb_re, bb_im = _s5_discretize(
        s5_log_dt[0], s5_a_re[0], s5_a_im[0],
        jnp.swapaxes(s5_b_re[0], 1, 2), jnp.swapaxes(s5_b_im[0], 1, 2))
    bw, cw = _s5_block_weights(bb_re, bb_im, s5_c_re[0], s5_c_im[0])
    wts = dict(
        lru=dict(w_in=bf(lru_w_in[0]), conv_w=lru_conv_w[0], conv_b=vec(lru_conv_b[0]),
                 wa=bf(lru_wa[0]), ba=vec(lru_ba[0]), wx=bf(lru_wx[0]), bx=vec(lru_bx[0]),
                 lam=vec(lru_lam[0]), w_out=bf(lru_w_out[0])),
        s5=dict(w_u=bf(s5_w_in[0, :, :e]), w_gate=bf(s5_w_in[0, :, e:]), bw=bw, cw=cw, ab_re=vec(ab_re), ab_im=vec(ab_im),
                d=vec(s5_d[0]), glu_w=bf(s5_glu_w[0]), glu_b=vec(s5_glu_b[0]), w_out=bf(s5_w_out[0])),
        rwkv=dict(mu=rwkv_mu[0], w_r=bf(hm(rwkv_w_r[0])), w_k=bf(hm(rwkv_w_k[0])),
                  w_v=bf(hm(rwkv_w_v[0])), w_g=bf(hm(rwkv_w_g[0])), w0=vec(hm(rwkv_w0[0])),
                  w1=bf(rwkv_w1[0]), w2=bf(hm(rwkv_w2[0])), a0=vec(hm(rwkv_a0[0])),
                  a1=bf(rwkv_a1[0]), a2=bf(hm(rwkv_a2[0])),
                  k_k=hm(rwkv_k_k[0]), k_a=hm(rwkv_k_a[0]), r_k=hm(rwkv_r_k[0]),
                  ln_w=hm(rwkv_ln_w[0]), ln_b=hm(rwkv_ln_b[0]), w_o=bf(hm(rwkv_w_o[0].T).T)),
        ret=dict(w_q=bf(ret_w_q[0]), w_k=bf(ret_w_k[0]), w_v=bf(ret_w_v[0]), w_g=bf(ret_w_g[0]),
                 w_o=bf(ret_w_o[0])),
    )

    meta = jnp.broadcast_to(meta_tokens[None].astype(x_prompt.dtype), (bp, N_META, d))
    hp = jnp.concatenate([meta, x_prompt], axis=1)
    xp = jnp.transpose(hp, (1, 0, 2)).reshape(t_p * bp, d)
    z = lambda *s: jnp.zeros(s, F32)
    p_states = dict(lru_conv=z(bp, LRU_CONV_W - 1, e), lru_h=z(bp, e),
                    s5_re=z(bp, S5_GROUPS, S5_STATE), s5_im=z(bp, S5_GROUPS, S5_STATE),
                    rwkv_shift=z(bp, d), rwkv_wkv=z(bp, RWKV_HEADS, RWKV_HEAD, RWKV_HEAD),
                    ret=z(bp, RET_HEADS, RET_DK, RET_DV))
    p_tiles = dict(lru_tt=_divisor_tile(t_p, 48, 1), s5_tt=_divisor_tile(t_p, 48, 1),
                   rwkv_tt=_divisor_tile(t_p, 24, 1), rec_tt=_divisor_tile(t_p, 8, 1),
                   rows=_divisor_tile(t_p * bp, 384, 8), ret_lc=RET_CHUNK,
                   ret_rows=_divisor_tile(bp * (-(-t_p // RET_CHUNK) * RET_CHUNK), 512, 8))
    yp, new_p = _run_stream(xp, p_states, wts, norm_pre, norm_post, nb=bp, t_len=t_p, pos0=0,
                            tiles=p_tiles)

    xs = jnp.transpose(x_sample, (1, 0, 2)).reshape(dec_seq * bs, d)
    s_states = dict(lru_conv=state_lru_conv[0], lru_h=state_lru_h[0], s5_re=state_s5_re[0],
                    s5_im=state_s5_im[0], rwkv_shift=state_rwkv_shift[0],
                    rwkv_wkv=state_rwkv_wkv[0], ret=state_ret[0])
    s_rows = _divisor_tile(dec_seq * bs, 256, 8)
    s_tiles = dict(lru_tt=2, s5_tt=1, rwkv_tt=2, rec_tt=dec_seq, rows=s_rows, ret_lc=dec_seq,
                   ret_rows=s_rows)
    ys, new_s = _run_stream(xs, s_states, wts, norm_pre, norm_post, nb=bs, t_len=dec_seq,
                            pos0=PAST_LEN, tiles=s_tiles)

    names = ("lru_conv", "lru_h", "s5_re", "s5_im", "rwkv_shift", "rwkv_wkv", "ret")
    return ((yp[:, N_META:], ys)
            + tuple(new_p[n][None] for n in names)
            + tuple(new_s[n][None] for n in names))
```

```python
import functools
import math

import jax
import jax.numpy as jnp
from jax import lax
from jax.experimental import pallas as pl
from jax.experimental.pallas import tpu as pltpu

F32 = jnp.float32
BF16 = jnp.bfloat16

D_MODEL = 1024
D_INNER = 2048
N_META = 16
NORM_EPS = 1e-6
LRU_CONV_W = 4
LRU_BLOCKS = 16
LRU_BLOCK = 128
LRU_C = 8.0
S5_GROUP = 16
S5_GROUPS = 128
S5_STATE = 64
S5_KB = 8
RWKV_HEAD = 64
RWKV_HEADS = 32
RWKV_LN_EPS = 64e-5
RET_HEADS = 4
RET_DK = 256
RET_DV = 512
ROPE_BASE = 10000.0
PAST_LEN = 16384
RET_CHUNK = 128

LANES = 128
VMEM_LIMIT = 56 * 1024 * 1024


def _params(*sem):
    return pltpu.CompilerParams(dimension_semantics=sem, vmem_limit_bytes=VMEM_LIMIT)


def _const(shape):
    zeros = (0,) * len(shape)
    return pl.BlockSpec(shape, lambda *_: zeros, pipeline_mode=pl.Buffered(1))


def _const_out(shape):
    zeros = (0,) * len(shape)
    return pl.BlockSpec(shape, lambda *_: zeros)


def _divisor_tile(n, pref, mult):
    best = mult
    t = mult
    while t <= min(n, pref):
        if n % t == 0:
            best = t
        t += mult
    assert n % best == 0
    return best


def _rms(x, g):
    return x * lax.rsqrt(jnp.mean(x * x, axis=-1, keepdims=True) + NORM_EPS) * g


def _mm(a, w):
    return jnp.dot(a.astype(BF16), w, preferred_element_type=F32)


def _silu(x):
    return x * jax.nn.sigmoid(x)


def _softplus(x):
    return jnp.maximum(x, 0.0) + jnp.log1p(jnp.exp(-jnp.abs(x)))


def _lru_kernel(x_ref, cs_ref, h0_ref, gpre_ref, gpost_ref, win_ref, cw_ref, cb_ref,
                wa_ref, ba_ref, wx_ref, bx_ref, lam_ref, wout_ref,
                xo_ref, cso_ref, ho_ref,
                uz_ref, b_ref, tail_ref, h_ref, *, nb, tt):
    rows = nb * tt
    e = D_INNER

    @pl.when(pl.program_id(0) == 0)
    def _():
        tail_ref[...] = cs_ref[...]
        h_ref[...] = h0_ref[...]

    xn = _rms(x_ref[...], gpre_ref[...])
    uz_ref[...] = _mm(xn, win_ref[...])

    for j in range(LRU_BLOCKS):
        sl = slice(j * LRU_BLOCK, (j + 1) * LRU_BLOCK)
        ext = jnp.concatenate([tail_ref[:, sl], uz_ref[:, sl]], axis=0)
        cw = cw_ref[:, sl]
        xc = cb_ref[:, sl]
        for jj in range(LRU_CONV_W):
            xc = xc + ext[jj * nb:jj * nb + rows] * cw[jj:jj + 1]
        tail_ref[:, sl] = ext[rows:rows + (LRU_CONV_W - 1) * nb]
        gate_r = jax.nn.sigmoid(_mm(xc, wa_ref[j]) + ba_ref[:, sl])
        gate_i = jax.nn.sigmoid(_mm(xc, wx_ref[j]) + bx_ref[:, sl])
        log_a = -LRU_C * gate_r * _softplus(-lam_ref[:, sl])
        a = jnp.exp(log_a)
        bx = jnp.sqrt(-jnp.tanh(log_a) * (a * a + 1.0)) * gate_i * xc
        h = h_ref[:, sl]
        hs = []
        for t in range(tt):
            h = a[t * nb:(t + 1) * nb] * h + bx[t * nb:(t + 1) * nb]
            hs.append(h)
        h_ref[:, sl] = h
        b_ref[:, sl] = jnp.concatenate(hs, axis=0)

    y = b_ref[...] * _silu(uz_ref[:, e:])
    out = _mm(y, wout_ref[...])
    xo_ref[...] = x_ref[...] + _rms(out, gpost_ref[...])
    cso_ref[...] = tail_ref[...]
    ho_ref[...] = h_ref[...]


def _lru_layer(x, conv_state, h0, gpre, gpost, w, *, nb, tt):
    r, d = x.shape
    e = D_INNER
    rows = nb * tt
    ctail = (LRU_CONV_W - 1) * nb
    row = lambda i: (i, 0)
    return pl.pallas_call(
        functools.partial(_lru_kernel, nb=nb, tt=tt),
        grid=(r // rows,),
        in_specs=[
            pl.BlockSpec((rows, d), row), _const((ctail, e)), _const((nb, e)),
            _const((1, d)), _const((1, d)), _const((d, 2 * e)),
            _const((LRU_CONV_W, e)), _const((1, e)),
            _const((LRU_BLOCKS, LRU_BLOCK, LRU_BLOCK)), _const((1, e)),
            _const((LRU_BLOCKS, LRU_BLOCK, LRU_BLOCK)), _const((1, e)),
            _const((1, e)), _const((e, d)),
        ],
        out_specs=[pl.BlockSpec((rows, d), row), _const_out((ctail, e)), _const_out((nb, e))],
        out_shape=[jax.ShapeDtypeStruct((r, d), F32),
                   jax.ShapeDtypeStruct((ctail, e), F32),
                   jax.ShapeDtypeStruct((nb, e), F32)],
        scratch_shapes=[pltpu.VMEM((rows, 2 * e), F32), pltpu.VMEM((rows, e), F32),
                        pltpu.VMEM((ctail, e), F32),
                        pltpu.VMEM((nb, e), F32)],
        compiler_params=_params("arbitrary"),
        name="lru_layer",
    )(x, conv_state, h0, gpre, gpost, w["w_in"], w["conv_w"], w["conv_b"],
      w["wa"], w["ba"], w["wx"], w["bx"], w["lam"], w["w_out"])


def _s5_disc_kernel(logdt_ref, are_ref, aim_ref, bre_ref, bim_ref,
                    abre_ref, abim_ref, bbre_ref, bbim_ref):
    dt = jnp.exp(logdt_ref[...])
    a_re = are_ref[...]
    a_im = aim_ref[...]
    mag = jnp.exp(dt * a_re)
    ang = dt * a_im
    ab_re = mag * jnp.cos(ang)
    ab_im = mag * jnp.sin(ang)
    den = a_re * a_re + a_im * a_im
    f_re = ((ab_re - 1.0) * a_re + ab_im * a_im) / den
    f_im = (ab_im * a_re - (ab_re - 1.0) * a_im) / den
    abre_ref[...] = ab_re
    abim_ref[...] = ab_im
    b_re = bre_ref[...]
    b_im = bim_ref[...]
    fr = f_re[:, None, :]
    fi = f_im[:, None, :]
    bbre_ref[...] = fr * b_re - fi * b_im
    bbim_ref[...] = fr * b_im + fi * b_re


def _s5_discretize(log_dt, a_re, a_im, b_re, b_im):
    g, n = a_re.shape
    c = b_re.shape[1]
    return pl.pallas_call(
        _s5_disc_kernel,
        out_shape=[jax.ShapeDtypeStruct((g, n), F32), jax.ShapeDtypeStruct((g, n), F32),
                   jax.ShapeDtypeStruct((g, c, n), F32), jax.ShapeDtypeStruct((g, c, n), F32)],
        name="s5_discretize",
    )(log_dt.reshape(g, 1), a_re, a_im, b_re, b_im)


def _gelu_tanh(x):
    return 0.5 * x * (1.0 + jnp.tanh(math.sqrt(2.0 / math.pi) * (x + 0.044715 * (x * x * x))))


def _s5_core_kernel(x_ref, gpre_ref, wu_ref, sre_ref, sim_ref, bw_ref, cw_ref, abre_ref, abim_ref,
                    d_ref, y_ref, sreo_ref, simo_ref,
                    xn_ref, bu_ref, xr_ref, xi_ref, *, nb, tt):
    gn = S5_GROUPS * S5_STATE // S5_KB
    ch = D_INNER // S5_KB

    @pl.when(pl.program_id(0) == 0)
    def _():
        xr_ref[...] = sre_ref[...]
        xi_ref[...] = sim_ref[...]

    xn_ref[...] = _rms(x_ref[...], gpre_ref[...]).astype(BF16)

    for kb in range(S5_KB):
        lanes = slice(kb * gn, (kb + 1) * gn)
        cols = slice(kb * ch, (kb + 1) * ch)
        u = jnp.dot(xn_ref[...], wu_ref[:, cols], preferred_element_type=F32)
        bu_ref[...] = _mm(u, bw_ref[kb])
        ar = abre_ref[:, lanes]
        ai = abim_ref[:, lanes]

        def step(t, carry):
            xr, xi = carry
            r = pl.ds(pl.multiple_of(t * nb, nb), nb)
            nr = ar * xr - ai * xi + bu_ref[r, :gn]
            ni = ar * xi + ai * xr + bu_ref[r, gn:]
            bu_ref[r, :gn] = nr
            bu_ref[r, gn:] = ni
            return nr, ni

        xr, xi = lax.fori_loop(0, tt, step, (xr_ref[:, lanes], xi_ref[:, lanes]), unroll=True)
        xr_ref[:, lanes] = xr
        xi_ref[:, lanes] = xi
        y = _mm(bu_ref[...], cw_ref[kb]) + d_ref[:, cols] * u
        y_ref[:, cols] = _gelu_tanh(y)

    sreo_ref[...] = xr_ref[...]
    simo_ref[...] = xi_ref[...]


def _s5_core(x, gpre, w, s_re, s_im, *, nb, tt):
    r, d = x.shape
    e = D_INNER
    gn = S5_GROUPS * S5_STATE
    rows = nb * tt
    row = lambda i: (i, 0)
    return pl.pallas_call(
        functools.partial(_s5_core_kernel, nb=nb, tt=tt),
        grid=(r // rows,),
        in_specs=[pl.BlockSpec((rows, d), row), _const((1, d)), _const((d, e)),
                  _const((nb, gn)), _const((nb, gn)), _const(w["bw"].shape), _const(w["cw"].shape),
                  _const((1, gn)), _const((1, gn)), _const((1, e))],
        out_specs=[pl.BlockSpec((rows, e), row), _const_out((nb, gn)), _const_out((nb, gn))],
        out_shape=[jax.ShapeDtypeStruct((r, e), F32), jax.ShapeDtypeStruct((nb, gn), F32),
                   jax.ShapeDtypeStruct((nb, gn), F32)],
        scratch_shapes=[pltpu.VMEM((rows, d), BF16), pltpu.VMEM((rows, 2 * gn // S5_KB), F32),
                        pltpu.VMEM((nb, gn), F32), pltpu.VMEM((nb, gn), F32)],
        compiler_params=_params("arbitrary"),
        name="s5_core",
    )(x, gpre, w["w_u"], s_re, s_im, w["bw"], w["cw"], w["ab_re"], w["ab_im"], w["d"])


def _s5_out_kernel(y_ref, x_ref, gpre_ref, wg_ref, gluw_ref, glub_ref, wout_ref, gpost_ref, xo_ref):
    x = x_ref[...]
    gate = _mm(_rms(x, gpre_ref[...]), wg_ref[...])
    y = y_ref[...]
    y = y * jax.nn.sigmoid(_mm(y, gluw_ref[...]) + glub_ref[...])
    y = y * _silu(gate)
    xo_ref[...] = x + _rms(_mm(y, wout_ref[...]), gpost_ref[...])


def _s5_out(y, x, gpre, gpost, w, *, rows):
    r, d = x.shape
    e = D_INNER
    row = lambda i: (i, 0)
    return pl.pallas_call(
        _s5_out_kernel,
        grid=(r // rows,),
        in_specs=[pl.BlockSpec((rows, e), row), pl.BlockSpec((rows, d), row), _const((1, d)),
                  _const((d, e)), _const((e, e)), _const((1, e)), _const((e, d)), _const((1, d))],
        out_specs=pl.BlockSpec((rows, d), row),
        out_shape=jax.ShapeDtypeStruct((r, d), F32),
        compiler_params=_params("parallel"),
        name="s5_out",
    )(y, x, gpre, w["w_gate"], w["glu_w"], w["glu_b"], w["w_out"], gpost)


def _rwkv_proj_kernel(x_ref, xprev_ref, gpre_ref, mu_ref, wr_ref, wk_ref, wv_ref, wg_ref,
                      w0_ref, w1_ref, w2_ref, a0_ref, a1_ref, a2_ref,
                      r_ref, k_ref, v_ref, g_ref, dec_ref, a_ref, shift_ref,
                      prev_ref, *, nb, tt):
    rows = nb * tt

    @pl.when(pl.program_id(0) == 0)
    def _():
        prev_ref[...] = xprev_ref[...]

    xn = _rms(x_ref[...], gpre_ref[...])
    if tt > 1:
        shifted = jnp.concatenate([prev_ref[...], xn[:rows - nb]], axis=0)
    else:
        shifted = prev_ref[...]
    prev_ref[...] = xn[rows - nb:]
    shift_ref[...] = xn[rows - nb:]
    xx = shifted - xn
    mix = lambda n: xn + xx * mu_ref[n:n + 1, :]
    r_ref[...] = _mm(mix(0), wr_ref[...])
    k_ref[...] = _mm(mix(2), wk_ref[...])
    v_ref[...] = _mm(mix(3), wv_ref[...])
    g_ref[...] = _silu(_mm(mix(5), wg_ref[...]))
    w_raw = w0_ref[...] + _mm(jnp.tanh(_mm(mix(1), w1_ref[...])), w2_ref[...])
    dec_ref[...] = jnp.exp(-(jax.nn.sigmoid(w_raw) * math.exp(-0.5)))
    a_ref[...] = jax.nn.sigmoid(a0_ref[...] + _mm(_mm(mix(4), a1_ref[...]), a2_ref[...]))


def _rwkv_proj(x, x_prev, gpre, w, *, nb, tt):
    r, d = x.shape
    e = D_INNER
    rows = nb * tt
    lora = w["w1"].shape[1]
    row = lambda i: (i, 0)
    big = pl.BlockSpec((rows, e), row)
    return pl.pallas_call(
        functools.partial(_rwkv_proj_kernel, nb=nb, tt=tt),
        grid=(r // rows,),
        in_specs=[pl.BlockSpec((rows, d), row), _const((nb, d)), _const((1, d)), _const((6, d)),
                  _const((d, e)), _const((d, e)), _const((d, e)), _const((d, e)),
                  _const((1, e)), _const((d, lora)), _const((lora, e)),
                  _const((1, e)), _const((d, lora)), _const((lora, e))],
        out_specs=[big, big, big, big, big, big, _const_out((nb, d))],
        out_shape=[jax.ShapeDtypeStruct((r, e), F32)] * 6 + [jax.ShapeDtypeStruct((nb, d), F32)],
        scratch_shapes=[pltpu.VMEM((nb, d), F32)],
        compiler_params=_params("arbitrary"),
        name="rwkv_proj",
    )(x, x_prev, gpre, w["mu"], w["w_r"], w["w_k"], w["w_v"], w["w_g"],
      w["w0"], w["w1"], w["w2"], w["a0"], w["a1"], w["a2"])


RWKV_VSUB = 4
RWKV_VGRP = RWKV_HEAD // RWKV_VSUB
RWKV_BG = 8
RWKV_VBLK = 8
RWKV_UNROLL = 32


def _seg_sum(x):
    lane_axis = x.ndim - 1
    x = x + pltpu.roll(x, 2 * RWKV_HEADS, lane_axis)
    return x + pltpu.roll(x, RWKV_HEADS, lane_axis)


def _rwkv_rec_kernel(r_ref, w_ref, k_ref, v_ref, a_ref, vprev_ref, kk_ref, ka_ref, rk_ref,
                     kkrep_ref, karep_ref, lnw_ref, lnb_ref, s0_ref, o_ref, so_ref,
                     s_ref, vec_ref, sa_scale_ref, bon_ref, yraw_ref, *, tt):
    n = RWKV_HEAD
    nh = RWKV_HEADS
    shape = (RWKV_BG, LANES)
    step = pl.program_id(1)
    last = pl.num_programs(1) - 1

    @pl.when(step == 0)
    def _():
        s_ref[...] = s0_ref[0]
        yraw_ref[...] = jnp.zeros_like(yraw_ref)
        bon_ref[...] = jnp.zeros_like(bon_ref)

    tiles = [slice(j * LANES, (j + 1) * LANES) for j in range(RWKV_VGRP)]

    bon_prev = bon_ref[...]
    tot = yraw_ref[:, :, tiles[0]]
    for sl in tiles[1:]:
        tot = tot + yraw_ref[:, :, sl]
    mean = _seg_sum(tot) * (1.0 / n)
    sq = jnp.zeros((tt,) + shape, F32)
    for sl in tiles:
        yc = yraw_ref[:, :, sl] - mean
        sq = sq + yc * yc
    rstd = lax.rsqrt(_seg_sum(sq) * (1.0 / n) + RWKV_LN_EPS)
    for sl in tiles:
        o_ref[:, :, sl] = ((yraw_ref[:, :, sl] - mean) * rstd * lnw_ref[:, sl] + lnb_ref[:, sl]
                           + bon_prev * vprev_ref[:, :, sl])

    seg = lax.broadcasted_iota(jnp.int32, (tt,) + shape, 2) // nh
    seg_low = seg < 2
    seg_odd = (seg & 1) == 1

    def rep4(x):
        rolled = [x] + [pltpu.roll(x, q * nh, 2) for q in range(1, RWKV_VSUB)]
        pair = [jnp.where(seg_low, rolled[i], rolled[(i + 2) % RWKV_VSUB]) for i in range(RWKV_VSUB)]
        return [jnp.where(seg_odd, pair[(1 - s_) % RWKV_VSUB], pair[(-s_) % RWKV_VSUB])
                for s_ in range(RWKV_VSUB)]

    n2 = jnp.zeros((tt,) + shape, F32)
    bon = jnp.zeros((tt,) + shape, F32)
    for j, sl in enumerate(tiles):
        r = r_ref[:, :, sl]
        k = k_ref[:, :, sl]
        a = a_ref[:, :, sl]
        kk = k * kk_ref[:, sl]
        n2 = n2 + kk * kk
        bon = bon + r * (k * (1.0 + (a - 1.0) * ka_ref[:, sl])) * rk_ref[:, sl]
        r_rep = rep4(r)
        w_rep = rep4(w_ref[:, :, sl])
        k_rep = rep4(k)
        a_rep = rep4(a)
        for s_ in range(RWKV_VSUB):
            c = RWKV_VSUB * j + s_
            kk_c = k_rep[s_] * kkrep_ref[c]
            vec_ref[:, 0, c] = r_rep[s_]
            vec_ref[:, 1, c] = w_rep[s_]
            vec_ref[:, 2, c] = kk_c
            vec_ref[:, 3, c] = kk_c * a_rep[s_]
            vec_ref[:, 4, c] = k_rep[s_] * (1.0 + (a_rep[s_] - 1.0) * karep_ref[c])
    inv = lax.rsqrt(jnp.maximum(_seg_sum(n2), 1e-24))
    sa_scale_ref[...] = -(inv * inv)
    bon_ref[...] = _seg_sum(bon)

    def token(t, carry):
        sa_scale = sa_scale_ref[t]
        zero = jnp.zeros((RWKV_VBLK,) + shape, F32)
        for vb in range(RWKV_VGRP // RWKV_VBLK):
            vsl = slice(vb * RWKV_VBLK, (vb + 1) * RWKV_VBLK)
            vv = jnp.stack([v_ref[t, :, sl] for sl in tiles[vsl]])

            def sa_step(c, acc):
                return acc + s_ref[c, vsl] * vec_ref[t, 2, c]

            sa = lax.fori_loop(0, n, sa_step, zero, unroll=RWKV_UNROLL) * sa_scale

            def update_step(c, acc):
                s_new = s_ref[c, vsl] * vec_ref[t, 1, c] + sa * vec_ref[t, 3, c] + vv * vec_ref[t, 4, c]
                s_ref[c, vsl] = s_new
                return acc + s_new * vec_ref[t, 0, c]

            y = lax.fori_loop(0, n, update_step, zero, unroll=RWKV_UNROLL)
            for i, sl in enumerate(tiles[vsl]):
                yraw_ref[t, :, sl] = y[i]
        return carry

    @pl.when(step < last)
    def _():
        lax.fori_loop(0, tt, token, 0)

    @pl.when(step == last)
    def _():
        so_ref[0] = s_ref[...]


def _rwkv_rec(r, w, k, v, a, pk, s0, *, tt):
    t_len, nb, e = r.shape
    n = RWKV_HEAD
    n_tiles = t_len // tt
    cur = pl.BlockSpec((tt, RWKV_BG, e), lambda bi, si: (jnp.minimum(si, n_tiles - 1), bi, 0))
    prev = pl.BlockSpec((tt, RWKV_BG, e), lambda bi, si: (jnp.maximum(si - 1, 0), bi, 0))
    st = pl.BlockSpec((1, n, RWKV_VGRP, RWKV_BG, LANES), lambda bi, si: (bi, 0, 0, 0, 0))
    return pl.pallas_call(
        functools.partial(_rwkv_rec_kernel, tt=tt),
        grid=(nb // RWKV_BG, n_tiles + 1),
        in_specs=[cur] * 5 + [prev] + [_const((RWKV_BG, e))] * 3
                 + [_const((n, RWKV_BG, LANES))] * 2 + [_const((RWKV_BG, e))] * 2 + [st],
        out_specs=[prev, st],
        out_shape=[jax.ShapeDtypeStruct((t_len, nb, e), F32), jax.ShapeDtypeStruct(s0.shape, F32)],
        scratch_shapes=[pltpu.VMEM((n, RWKV_VGRP, RWKV_BG, LANES), F32),
                        pltpu.VMEM((tt, 5, n, RWKV_BG, LANES), F32),
                        pltpu.VMEM((tt, RWKV_BG, LANES), F32),
                        pltpu.VMEM((tt, RWKV_BG, LANES), F32),
                        pltpu.VMEM((tt, RWKV_BG, e), F32)],
        compiler_params=_params("parallel", "arbitrary"),
        name="rwkv_recurrence",
    )(r, w, k, v, a, v, pk["k_k"], pk["k_a"], pk["r_k"], pk["k_k_rep"], pk["k_a_rep"],
      pk["ln_w"], pk["ln_b"], s0)


def _gated_out_kernel(y_ref, g_ref, x_ref, w_ref, gpost_ref, xo_ref):
    y = y_ref[...] * g_ref[...]
    xo_ref[...] = x_ref[...] + _rms(_mm(y, w_ref[...]), gpost_ref[...])


def _gated_out(y, g, x, w, gpost, *, rows):
    r, d = x.shape
    e = D_INNER
    row = lambda i: (i, 0)
    return pl.pallas_call(
        _gated_out_kernel,
        grid=(r // rows,),
        in_specs=[pl.BlockSpec((rows, e), row), pl.BlockSpec((rows, e), row),
                  pl.BlockSpec((rows, d), row), _const((e, d)), _const((1, d))],
        out_specs=pl.BlockSpec((rows, d), row),
        out_shape=jax.ShapeDtypeStruct((r, d), F32),
        compiler_params=_params("parallel"),
        name="gated_out",
    )(y, g, x, w, gpost)


def _ret_proj_kernel(x_ref, gpre_ref, wq_ref, wk_ref, wv_ref, wg_ref, q_ref, k_ref, v_ref, g_ref):
    xn = _rms(x_ref[...], gpre_ref[...]).astype(BF16)
    q_ref[...] = jnp.dot(xn, wq_ref[...], preferred_element_type=F32)
    k_ref[...] = jnp.dot(xn, wk_ref[...], preferred_element_type=F32)
    v_ref[...] = jnp.dot(xn, wv_ref[...], preferred_element_type=F32)
    g_ref[...] = _silu(jnp.dot(xn, wg_ref[...], preferred_element_type=F32))


def _ret_proj(x, gpre, w, *, rows):
    r, d = x.shape
    e = D_INNER
    row = lambda i: (i, 0)
    return pl.pallas_call(
        _ret_proj_kernel,
        grid=(r // rows,),
        in_specs=[pl.BlockSpec((rows, d), row), _const((1, d)), _const((d, d)), _const((d, d)),
                  _const((d, e)), _const((d, e))],
        out_specs=[pl.BlockSpec((rows, d), row), pl.BlockSpec((rows, d), row),
                   pl.BlockSpec((rows, e), row), pl.BlockSpec((rows, e), row)],
        out_shape=[jax.ShapeDtypeStruct((r, d), F32), jax.ShapeDtypeStruct((r, d), F32),
                   jax.ShapeDtypeStruct((r, e), F32), jax.ShapeDtypeStruct((r, e), F32)],
        compiler_params=_params("parallel"),
        name="ret_proj",
    )(x, gpre, w["w_q"], w["w_k"], w["w_v"], w["w_g"])


def _ret_core_kernel(q_ref, kt_ref, v_ref, cos_ref, sin_ref, cost_ref, sint_ref, mask_ref,
                     qdec_ref, kdec_ref, gl_ref, s0_ref, *rest, fuse_out):
    half = RET_DK // 2
    if fuse_out:
        g_ref, x_ref, wo_ref, gpost_ref, out_ref, so_ref, s_ref = rest
    else:
        out_ref, so_ref, s_ref = rest

    @pl.when(pl.program_id(1) == 0)
    def _():
        s_ref[...] = s0_ref[0]

    cos = cos_ref[...]
    sin = sin_ref[...]
    cost = cost_ref[...]
    sint = sint_ref[...]
    proj = None
    for h in range(RET_HEADS):
        q1 = q_ref[:, h * RET_DK:h * RET_DK + half]
        q2 = q_ref[:, h * RET_DK + half:(h + 1) * RET_DK]
        qr = jnp.concatenate([q1 * cos - q2 * sin, q2 * cos + q1 * sin], axis=1)
        k1 = kt_ref[0, h * RET_DK:h * RET_DK + half, :]
        k2 = kt_ref[0, h * RET_DK + half:(h + 1) * RET_DK, :]
        krt = jnp.concatenate([k1 * cost - k2 * sint, k2 * cost + k1 * sint], axis=0)
        krt = krt * (RET_DK ** -0.5)
        vh = v_ref[:, h * RET_DV:(h + 1) * RET_DV].astype(BF16)
        qb = qr.astype(BF16)
        scores = jnp.dot(qb, krt.astype(BF16), preferred_element_type=F32) * mask_ref[h]
        s_old = s_ref[h]
        y = jnp.dot(scores.astype(BF16), vh, preferred_element_type=F32)
        y = y + jnp.dot(qb, s_old.astype(BF16), preferred_element_type=F32) * qdec_ref[h]
        kw = (krt * kdec_ref[h]).astype(BF16)
        s_ref[h] = gl_ref[h] * s_old + jnp.dot(kw, vh, preferred_element_type=F32)
        y = y * lax.rsqrt(jnp.mean(y * y, axis=-1, keepdims=True) + NORM_EPS)
        if fuse_out:
            yg = y * g_ref[:, h * RET_DV:(h + 1) * RET_DV]
            part = _mm(yg, wo_ref[h * RET_DV:(h + 1) * RET_DV, :])
            proj = part if proj is None else proj + part
        else:
            out_ref[:, h * RET_DV:(h + 1) * RET_DV] = y
    if fuse_out:
        out_ref[...] = x_ref[...] + _rms(proj, gpost_ref[...])

    @pl.when(pl.program_id(1) == pl.num_programs(1) - 1)
    def _():
        so_ref[0] = s_ref[...]


def _ret_core(q, kt, v, tabs, s0, *, lc, out=None):
    b, dkh, t_len = kt.shape
    nc = t_len // lc
    e = D_INNER
    d = D_MODEL
    rowc = lambda bi, ci: (bi * nc + ci, 0)
    state = pl.BlockSpec((1, RET_HEADS, RET_DK, RET_DV), lambda bi, ci: (bi, 0, 0, 0))
    half = RET_DK // 2
    in_specs = [pl.BlockSpec((lc, dkh), rowc),
                pl.BlockSpec((1, dkh, lc), lambda bi, ci: (bi, 0, ci)),
                pl.BlockSpec((lc, e), rowc),
                pl.BlockSpec((lc, half), lambda bi, ci: (ci, 0)),
                pl.BlockSpec((lc, half), lambda bi, ci: (ci, 0)),
                pl.BlockSpec((half, lc), lambda bi, ci: (0, ci)),
                pl.BlockSpec((half, lc), lambda bi, ci: (0, ci)),
                _const((RET_HEADS, lc, lc)), _const((RET_HEADS, lc, 1)),
                _const((RET_HEADS, 1, lc)), _const((RET_HEADS, 1, 1)), state]
    args = [q, kt, v, tabs["cos"], tabs["sin"], tabs["cos_t"], tabs["sin_t"], tabs["mask"],
            tabs["qdec"], tabs["kdec"], tabs["gl"], s0]
    width = e
    if out is not None:
        in_specs += [pl.BlockSpec((lc, e), rowc), pl.BlockSpec((lc, d), rowc), _const((e, d)),
                     _const((1, d))]
        args += list(out)
        width = d
    return pl.pallas_call(
        functools.partial(_ret_core_kernel, fuse_out=out is not None),
        grid=(b, nc),
        in_specs=in_specs,
        out_specs=[pl.BlockSpec((lc, width), rowc), state],
        out_shape=[jax.ShapeDtypeStruct((b * t_len, width), F32),
                   jax.ShapeDtypeStruct((b, RET_HEADS, RET_DK, RET_DV), F32)],
        scratch_shapes=[pltpu.VMEM((RET_HEADS, RET_DK, RET_DV), F32)],
        compiler_params=_params("parallel", "arbitrary"),
        name="ret_core",
    )(*args)


def _ret_tables(pos, lc):
    half = RET_DK // 2
    inv = ROPE_BASE ** (-jnp.arange(half, dtype=F32) / half)
    ang = pos.astype(F32)[:, None] * inv[None, :]
    cos = jnp.cos(ang)
    sin = jnp.sin(ang)
    log_g = jnp.log1p(-jnp.exp2(-5.0 - jnp.arange(RET_HEADS, dtype=F32)))
    n = jnp.arange(lc, dtype=F32)
    diff = n[:, None] - n[None, :]
    mask = jnp.where(diff[None] >= 0, jnp.exp(diff[None] * log_g[:, None, None]), 0.0)
    qdec = jnp.exp((n[None, :] + 1.0) * log_g[:, None])[:, :, None]
    kdec = jnp.exp((lc - 1.0 - n)[None, :] * log_g[:, None])[:, None, :]
    gl = jnp.exp(lc * log_g)[:, None, None]
    return dict(cos=cos, sin=sin, cos_t=cos.T, sin_t=sin.T, mask=mask, qdec=qdec, kdec=kdec, gl=gl)


def _s5_block_weights(bb_re, bb_im, c_re, c_im):
    gl = S5_GROUPS // S5_KB
    eye = jnp.eye(gl, dtype=F32)

    def bdiag_b(bb):
        x = bb.reshape(S5_KB, gl, S5_GROUP, S5_STATE)
        x = x[:, :, :, None, :] * eye[None, :, None, :, None]
        return x.reshape(S5_KB, gl * S5_GROUP, gl * S5_STATE)

    def bdiag_c(c):
        x = jnp.swapaxes(c, 1, 2).reshape(S5_KB, gl, S5_STATE, S5_GROUP)
        x = x[:, :, :, None, :] * eye[None, :, None, :, None]
        return x.reshape(S5_KB, gl * S5_STATE, gl * S5_GROUP)

    bw = jnp.concatenate([bdiag_b(bb_re), bdiag_b(bb_im)], axis=2).astype(BF16)
    cw = jnp.concatenate([bdiag_c(c_re), -bdiag_c(c_im)], axis=1).astype(BF16)
    return bw, cw


def _head_minor(x):
    lead = x.shape[:-1]
    x = x.reshape(lead + (RWKV_HEADS, RWKV_HEAD))
    return jnp.swapaxes(x, -1, -2).reshape(lead + (D_INNER,))


def _key_param(p):
    return jnp.broadcast_to(p.reshape(1, D_INNER), (RWKV_BG, D_INNER))


def _key_param_rep(p):
    x = jnp.tile(p.reshape(RWKV_HEAD, 1, RWKV_HEADS), (1, 1, RWKV_VSUB))
    return jnp.broadcast_to(x, (RWKV_HEAD, RWKV_BG, LANES))


def _wkv_to_blocks(s, nb):
    x = s.reshape(nb // RWKV_BG, RWKV_BG, RWKV_HEADS, RWKV_VGRP, RWKV_VSUB, RWKV_HEAD)
    x = jnp.transpose(x, (0, 5, 3, 1, 4, 2))
    return x.reshape(nb // RWKV_BG, RWKV_HEAD, RWKV_VGRP, RWKV_BG, LANES)


def _wkv_from_blocks(x, nb):
    x = x.reshape(nb // RWKV_BG, RWKV_HEAD, RWKV_VGRP, RWKV_BG, RWKV_VSUB, RWKV_HEADS)
    x = jnp.transpose(x, (0, 3, 5, 2, 4, 1))
    return x.reshape(nb, RWKV_HEADS, RWKV_HEAD, RWKV_HEAD)


def _run_stream(x, states, wts, norm_pre, norm_post, *, nb, t_len, pos0, tiles):
    d = D_MODEL
    e = D_INNER
    new = {}
    gp = lambda i: (norm_pre[i].reshape(1, d), norm_post[i].reshape(1, d))

    gpre, gpost = gp(0)
    cs = jnp.transpose(states["lru_conv"], (1, 0, 2)).reshape((LRU_CONV_W - 1) * nb, e)
    x, cso, ho = _lru_layer(x, cs, states["lru_h"], gpre, gpost, wts["lru"], nb=nb, tt=tiles["lru_tt"])
    new["lru_conv"] = jnp.transpose(cso.reshape(LRU_CONV_W - 1, nb, e), (1, 0, 2))
    new["lru_h"] = ho

    gpre, gpost = gp(1)
    w = wts["s5"]
    gn = S5_GROUPS * S5_STATE
    y, sre, sim = _s5_core(x, gpre, w, states["s5_re"].reshape(nb, gn),
                           states["s5_im"].reshape(nb, gn), nb=nb, tt=tiles["s5_tt"])
    x = _s5_out(y, x, gpre, gpost, w, rows=tiles["rows"])
    new["s5_re"] = sre.reshape(nb, S5_GROUPS, S5_STATE)
    new["s5_im"] = sim.reshape(nb, S5_GROUPS, S5_STATE)

    gpre, gpost = gp(2)
    w = wts["rwkv"]
    r, k, v, g, dec, a, shift = _rwkv_proj(x, states["rwkv_shift"], gpre, w, nb=nb, tt=tiles["rwkv_tt"])
    as3 = lambda z: z.reshape(t_len, nb, e)
    pk = dict(k_k=_key_param(w["k_k"]), k_a=_key_param(w["k_a"]), r_k=_key_param(w["r_k"]),
              k_k_rep=_key_param_rep(w["k_k"]), k_a_rep=_key_param_rep(w["k_a"]),
              ln_w=_key_param(w["ln_w"]), ln_b=_key_param(w["ln_b"]))
    o, s_new = _rwkv_rec(as3(r), as3(dec), as3(k), as3(v), as3(a), pk,
                         _wkv_to_blocks(states["rwkv_wkv"], nb), tt=tiles["rec_tt"])
    x = _gated_out(o.reshape(t_len * nb, e), g, x, w["w_o"], gpost, rows=tiles["rows"])
    new["rwkv_shift"] = shift
    new["rwkv_wkv"] = _wkv_from_blocks(s_new, nb)

    gpre, gpost = gp(3)
    w = wts["ret"]
    lc = tiles["ret_lc"]
    t_pad = -(-t_len // lc) * lc
    pad = t_pad - t_len
    xb = jnp.transpose(x.reshape(t_len, nb, d), (1, 0, 2))
    xb = jnp.pad(xb, ((0, 0), (pad, 0), (0, 0))).reshape(nb * t_pad, d)
    q, kk, v, g = _ret_proj(xb, gpre, w, rows=tiles["ret_rows"])
    kt = jnp.transpose(kk.reshape(nb, t_pad, RET_HEADS * RET_DK), (0, 2, 1))
    tabs = _ret_tables(pos0 - pad + jnp.arange(t_pad, dtype=jnp.int32), lc)
    if lc % LANES == 0:
        xb, s_new = _ret_core(q, kt, v, tabs, states["ret"], lc=lc, out=(g, xb, w["w_o"], gpost))
    else:
        y, s_new = _ret_core(q, kt, v, tabs, states["ret"], lc=lc)
        xb = _gated_out(y, g, xb, w["w_o"], gpost, rows=tiles["ret_rows"])
    new["ret"] = s_new
    xb = xb.reshape(nb, t_pad, d)[:, pad:]
    return xb, new


def kernel(x_prompt, x_sample, state_lru_conv, state_lru_h, state_s5_re, state_s5_im, state_rwkv_shift, state_rwkv_wkv, state_ret, meta_tokens, norm_pre, norm_post, lru_w_in, lru_conv_w, lru_conv_b, lru_wa, lru_ba, lru_wx, lru_bx, lru_lam, lru_w_out, s5_w_in, s5_log_dt, s5_a_re, s5_a_im, s5_b_re, s5_b_im, s5_c_re, s5_c_im, s5_d, s5_glu_w, s5_glu_b, s5_w_out, rwkv_mu, rwkv_w_r, rwkv_w_k, rwkv_w_v, rwkv_w_g, rwkv_w0, rwkv_w1, rwkv_w2, rwkv_a0, rwkv_a1, rwkv_a2, rwkv_k_k, rwkv_k_a, rwkv_r_k, rwkv_ln_w, rwkv_ln_b, rwkv_w_o, ret_w_q, ret_w_k, ret_w_v, ret_w_g, ret_w_o):
    d = D_MODEL
    e = D_INNER
    bp, seq, _ = x_prompt.shape
    bs, dec_seq, _ = x_sample.shape
    t_p = seq + N_META
    bf = lambda x: x.astype(BF16)
    vec = lambda x: x.reshape(1, -1)
    hm = _head_minor

    ab_re, ab_im, bb_re, bb_im = _s5_discretize(
        s5_log_dt[0], s5_a_re[0], s5_a_im[0],
        jnp.swapaxes(s5_b_re[0], 1, 2), jnp.swapaxes(s5_b_im[0], 1, 2))
    bw, cw = _s5_block_weights(bb_re, bb_im, s5_c_re[0], s5_c_im[0])
    wts = dict(
        lru=dict(w_in=bf(lru_w_in[0]), conv_w=lru_conv_w[0], conv_b=vec(lru_conv_b[0]),
                 wa=bf(lru_wa[0]), ba=vec(lru_ba[0]), wx=bf(lru_wx[0]), bx=vec(lru_bx[0]),
                 lam=vec(lru_lam[0]), w_out=bf(lru_w_out[0])),
        s5=dict(w_u=bf(s5_w_in[0, :, :e]), w_gate=bf(s5_w_in[0, :, e:]), bw=bw, cw=cw, ab_re=vec(ab_re), ab_im=vec(ab_im),
                d=vec(s5_d[0]), glu_w=bf(s5_glu_w[0]), glu_b=vec(s5_glu_b[0]), w_out=bf(s5_w_out[0])),
        rwkv=dict(mu=rwkv_mu[0], w_r=bf(hm(rwkv_w_r[0])), w_k=bf(hm(rwkv_w_k[0])),
                  w_v=bf(hm(rwkv_w_v[0])), w_g=bf(hm(rwkv_w_g[0])), w0=vec(hm(rwkv_w0[0])),
                  w1=bf(rwkv_w1[0]), w2=bf(hm(rwkv_w2[0])), a0=vec(hm(rwkv_a0[0])),
                  a1=bf(rwkv_a1[0]), a2=bf(hm(rwkv_a2[0])),
                  k_k=hm(rwkv_k_k[0]), k_a=hm(rwkv_k_a[0]), r_k=hm(rwkv_r_k[0]),
                  ln_w=hm(rwkv_ln_w[0]), ln_b=hm(rwkv_ln_b[0]), w_o=bf(hm(rwkv_w_o[0].T).T)),
        ret=dict(w_q=bf(ret_w_q[0]), w_k=bf(ret_w_k[0]), w_v=bf(ret_w_v[0]), w_g=bf(ret_w_g[0]),
                 w_o=bf(ret_w_o[0])),
    )

    meta = jnp.broadcast_to(meta_tokens[None].astype(x_prompt.dtype), (bp, N_META, d))
    hp = jnp.concatenate([meta, x_prompt], axis=1)
    xp = jnp.transpose(hp, (1, 0, 2)).reshape(t_p * bp, d)
    z = lambda *s: jnp.zeros(s, F32)
    p_states = dict(lru_conv=z(bp, LRU_CONV_W - 1, e), lru_h=z(bp, e),
                    s5_re=z(bp, S5_GROUPS, S5_STATE), s5_im=z(bp, S5_GROUPS, S5_STATE),
                    rwkv_shift=z(bp, d), rwkv_wkv=z(bp, RWKV_HEADS, RWKV_HEAD, RWKV_HEAD),
                    ret=z(bp, RET_HEADS, RET_DK, RET_DV))
    p_tiles = dict(lru_tt=_divisor_tile(t_p, 48, 1), s5_tt=_divisor_tile(t_p, 48, 1),
                   rwkv_tt=_divisor_tile(t_p, 24, 1), rec_tt=_divisor_tile(t_p, 16, 1),
                   rows=_divisor_tile(t_p * bp, 384, 8), ret_lc=RET_CHUNK,
                   ret_rows=_divisor_tile(bp * (-(-t_p // RET_CHUNK) * RET_CHUNK), 512, 8))
    yp, new_p = _run_stream(xp, p_states, wts, norm_pre, norm_post, nb=bp, t_len=t_p, pos0=0,
                            tiles=p_tiles)

    xs = jnp.transpose(x_sample, (1, 0, 2)).reshape(dec_seq * bs, d)
    s_states = dict(lru_conv=state_lru_conv[0], lru_h=state_lru_h[0], s5_re=state_s5_re[0],
                    s5_im=state_s5_im[0], rwkv_shift=state_rwkv_shift[0],
                    rwkv_wkv=state_rwkv_wkv[0], ret=state_ret[0])
    s_rows = _divisor_tile(dec_seq * bs, 256, 8)
    s_tiles = dict(lru_tt=2, s5_tt=1, rwkv_tt=2, rec_tt=dec_seq, rows=s_rows, ret_lc=dec_seq,
                   ret_rows=s_rows)
    ys, new_s = _run_stream(xs, s_states, wts, norm_pre, norm_post, nb=bs, t_len=dec_seq,
                            pos0=PAST_LEN, tiles=s_tiles)

    names = ("lru_conv", "lru_h", "s5_re", "s5_im", "rwkv_shift", "rwkv_wkv", "ret")
    return ((yp[:, N_META:], ys)
            + tuple(new_p[n][None] for n in names)
            + tuple(new_s[n][None] for n in names))
```

```python
import functools
import math

import jax
import jax.numpy as jnp
from jax import lax
from jax.experimental import pallas as pl
from jax.experimental.pallas import tpu as pltpu

F32 = jnp.float32
BF16 = jnp.bfloat16

D_MODEL = 1024
D_INNER = 2048
N_META = 16
NORM_EPS = 1e-6
LRU_CONV_W = 4
LRU_BLOCKS = 16
LRU_BLOCK = 128
LRU_C = 8.0
S5_GROUP = 16
S5_GROUPS = 128
S5_STATE = 64
S5_KB = 8
RWKV_HEAD = 64
RWKV_HEADS = 32
RWKV_LN_EPS = 64e-5
RET_HEADS = 4
RET_DK = 256
RET_DV = 512
ROPE_BASE = 10000.0
PAST_LEN = 16384
RET_CHUNK = 128

LANES = 128
VMEM_LIMIT = 56 * 1024 * 1024


def _params(*sem):
    return pltpu.CompilerParams(dimension_semantics=sem, vmem_limit_bytes=VMEM_LIMIT)


def _const(shape):
    zeros = (0,) * len(shape)
    return pl.BlockSpec(shape, lambda *_: zeros, pipeline_mode=pl.Buffered(1))


def _const_out(shape):
    zeros = (0,) * len(shape)
    return pl.BlockSpec(shape, lambda *_: zeros)


def _divisor_tile(n, pref, mult):
    best = mult
    t = mult
    while t <= min(n, pref):
        if n % t == 0:
            best = t
        t += mult
    assert n % best == 0
    return best


def _rms(x, g):
    return x * lax.rsqrt(jnp.mean(x * x, axis=-1, keepdims=True) + NORM_EPS) * g


def _mm(a, w):
    return jnp.dot(a.astype(BF16), w, preferred_element_type=F32)


def _silu(x):
    return x * jax.nn.sigmoid(x)


def _softplus(x):
    return jnp.maximum(x, 0.0) + jnp.log1p(jnp.exp(-jnp.abs(x)))


def _lru_kernel(x_ref, cs_ref, h0_ref, gpre_ref, gpost_ref, win_ref, cw_ref, cb_ref,
                wa_ref, ba_ref, wx_ref, bx_ref, lam_ref, wout_ref,
                xo_ref, cso_ref, ho_ref,
                uz_ref, b_ref, tail_ref, h_ref, *, nb, tt):
    rows = nb * tt
    e = D_INNER

    @pl.when(pl.program_id(0) == 0)
    def _():
        tail_ref[...] = cs_ref[...]
        h_ref[...] = h0_ref[...]

    d = D_MODEL
    x = jnp.concatenate([x_ref[:, t * d:(t + 1) * d] for t in range(tt)], axis=0)
    xn = _rms(x, gpre_ref[...])
    uz_ref[...] = _mm(xn, win_ref[...])

    for j in range(LRU_BLOCKS):
        sl = slice(j * LRU_BLOCK, (j + 1) * LRU_BLOCK)
        ext = jnp.concatenate([tail_ref[:, sl], uz_ref[:, sl]], axis=0)
        cw = cw_ref[:, sl]
        xc = cb_ref[:, sl]
        for jj in range(LRU_CONV_W):
            xc = xc + ext[jj * nb:jj * nb + rows] * cw[jj:jj + 1]
        tail_ref[:, sl] = ext[rows:rows + (LRU_CONV_W - 1) * nb]
        gate_r = jax.nn.sigmoid(_mm(xc, wa_ref[j]) + ba_ref[:, sl])
        gate_i = jax.nn.sigmoid(_mm(xc, wx_ref[j]) + bx_ref[:, sl])
        log_a = -LRU_C * gate_r * _softplus(-lam_ref[:, sl])
        a = jnp.exp(log_a)
        bx = jnp.sqrt(-jnp.tanh(log_a) * (a * a + 1.0)) * gate_i * xc
        h = h_ref[:, sl]
        hs = []
        for t in range(tt):
            h = a[t * nb:(t + 1) * nb] * h + bx[t * nb:(t + 1) * nb]
            hs.append(h)
        h_ref[:, sl] = h
        b_ref[:, sl] = jnp.concatenate(hs, axis=0)

    y = b_ref[...] * _silu(uz_ref[:, e:])
    out = _mm(y, wout_ref[...])
    xo_ref[...] = x + _rms(out, gpost_ref[...])
    cso_ref[...] = tail_ref[...]
    ho_ref[...] = h_ref[...]


def _lru_layer(x, conv_state, h0, gpre, gpost, w, *, nb, tt):
    _, t_len, d = x.shape
    r = nb * t_len
    e = D_INNER
    rows = nb * tt
    ctail = (LRU_CONV_W - 1) * nb
    row = lambda i: (i, 0)
    return pl.pallas_call(
        functools.partial(_lru_kernel, nb=nb, tt=tt),
        grid=(r // rows,),
        in_specs=[
            pl.BlockSpec((nb, tt * d), lambda i: (0, i)), _const((ctail, e)), _const((nb, e)),
            _const((1, d)), _const((1, d)), _const((d, 2 * e)),
            _const((LRU_CONV_W, e)), _const((1, e)),
            _const((LRU_BLOCKS, LRU_BLOCK, LRU_BLOCK)), _const((1, e)),
            _const((LRU_BLOCKS, LRU_BLOCK, LRU_BLOCK)), _const((1, e)),
            _const((1, e)), _const((e, d)),
        ],
        out_specs=[pl.BlockSpec((rows, d), row), _const_out((ctail, e)), _const_out((nb, e))],
        out_shape=[jax.ShapeDtypeStruct((r, d), F32),
                   jax.ShapeDtypeStruct((ctail, e), F32),
                   jax.ShapeDtypeStruct((nb, e), F32)],
        scratch_shapes=[pltpu.VMEM((rows, 2 * e), F32), pltpu.VMEM((rows, e), F32),
                        pltpu.VMEM((ctail, e), F32),
                        pltpu.VMEM((nb, e), F32)],
        compiler_params=_params("arbitrary"),
        name="lru_layer",
    )(x.reshape(nb, t_len * d), conv_state, h0, gpre, gpost, w["w_in"], w["conv_w"], w["conv_b"],
      w["wa"], w["ba"], w["wx"], w["bx"], w["lam"], w["w_out"])


def _s5_disc_kernel(logdt_ref, are_ref, aim_ref, bre_ref, bim_ref,
                    abre_ref, abim_ref, bbre_ref, bbim_ref):
    dt = jnp.exp(logdt_ref[...])
    a_re = are_ref[...]
    a_im = aim_ref[...]
    mag = jnp.exp(dt * a_re)
    ang = dt * a_im
    ab_re = mag * jnp.cos(ang)
    ab_im = mag * jnp.sin(ang)
    den = a_re * a_re + a_im * a_im
    f_re = ((ab_re - 1.0) * a_re + ab_im * a_im) / den
    f_im = (ab_im * a_re - (ab_re - 1.0) * a_im) / den
    abre_ref[...] = ab_re
    abim_ref[...] = ab_im
    b_re = bre_ref[...]
    b_im = bim_ref[...]
    fr = f_re[:, None, :]
    fi = f_im[:, None, :]
    bbre_ref[...] = fr * b_re - fi * b_im
    bbim_ref[...] = fr * b_im + fi * b_re


def _s5_discretize(log_dt, a_re, a_im, b_re, b_im):
    g, n = a_re.shape
    c = b_re.shape[1]
    return pl.pallas_call(
        _s5_disc_kernel,
        out_shape=[jax.ShapeDtypeStruct((g, n), F32), jax.ShapeDtypeStruct((g, n), F32),
                   jax.ShapeDtypeStruct((g, c, n), F32), jax.ShapeDtypeStruct((g, c, n), F32)],
        name="s5_discretize",
    )(log_dt.reshape(g, 1), a_re, a_im, b_re, b_im)


def _gelu_tanh(x):
    return 0.5 * x * (1.0 + jnp.tanh(math.sqrt(2.0 / math.pi) * (x + 0.044715 * (x * x * x))))


def _s5_core_kernel(x_ref, gpre_ref, wu_ref, sre_ref, sim_ref, bw_ref, cw_ref, abre_ref, abim_ref,
                    d_ref, y_ref, sreo_ref, simo_ref,
                    xn_ref, bu_ref, xr_ref, xi_ref, *, nb, tt):
    gn = S5_GROUPS * S5_STATE // S5_KB
    ch = D_INNER // S5_KB

    @pl.when(pl.program_id(0) == 0)
    def _():
        xr_ref[...] = sre_ref[...]
        xi_ref[...] = sim_ref[...]

    xn_ref[...] = _rms(x_ref[...], gpre_ref[...]).astype(BF16)

    for kb in range(S5_KB):
        lanes = slice(kb * gn, (kb + 1) * gn)
        cols = slice(kb * ch, (kb + 1) * ch)
        u = jnp.dot(xn_ref[...], wu_ref[:, cols], preferred_element_type=F32)
        bu_ref[...] = _mm(u, bw_ref[kb])
        ar = abre_ref[:, lanes]
        ai = abim_ref[:, lanes]

        def step(t, carry):
            xr, xi = carry
            r = pl.ds(pl.multiple_of(t * nb, nb), nb)
            nr = ar * xr - ai * xi + bu_ref[r, :gn]
            ni = ar * xi + ai * xr + bu_ref[r, gn:]
            bu_ref[r, :gn] = nr
            bu_ref[r, gn:] = ni
            return nr, ni

        xr, xi = lax.fori_loop(0, tt, step, (xr_ref[:, lanes], xi_ref[:, lanes]), unroll=True)
        xr_ref[:, lanes] = xr
        xi_ref[:, lanes] = xi
        y = _mm(bu_ref[...], cw_ref[kb]) + d_ref[:, cols] * u
        y_ref[:, cols] = _gelu_tanh(y)

    sreo_ref[...] = xr_ref[...]
    simo_ref[...] = xi_ref[...]


def _s5_core(x, gpre, w, s_re, s_im, *, nb, tt):
    r, d = x.shape
    e = D_INNER
    gn = S5_GROUPS * S5_STATE
    rows = nb * tt
    row = lambda i: (i, 0)
    return pl.pallas_call(
        functools.partial(_s5_core_kernel, nb=nb, tt=tt),
        grid=(r // rows,),
        in_specs=[pl.BlockSpec((rows, d), row), _const((1, d)), _const((d, e)),
                  _const((nb, gn)), _const((nb, gn)), _const(w["bw"].shape), _const(w["cw"].shape),
                  _const((1, gn)), _const((1, gn)), _const((1, e))],
        out_specs=[pl.BlockSpec((rows, e), row), _const_out((nb, gn)), _const_out((nb, gn))],
        out_shape=[jax.ShapeDtypeStruct((r, e), F32), jax.ShapeDtypeStruct((nb, gn), F32),
                   jax.ShapeDtypeStruct((nb, gn), F32)],
        scratch_shapes=[pltpu.VMEM((rows, d), BF16), pltpu.VMEM((rows, 2 * gn // S5_KB), F32),
                        pltpu.VMEM((nb, gn), F32), pltpu.VMEM((nb, gn), F32)],
        compiler_params=_params("arbitrary"),
        name="s5_core",
    )(x, gpre, w["w_u"], s_re, s_im, w["bw"], w["cw"], w["ab_re"], w["ab_im"], w["d"])


def _s5_out_kernel(y_ref, x_ref, gpre_ref, wg_ref, gluw_ref, glub_ref, wout_ref, gpost_ref, xo_ref):
    x = x_ref[...]
    gate = _mm(_rms(x, gpre_ref[...]), wg_ref[...])
    y = y_ref[...]
    y = y * jax.nn.sigmoid(_mm(y, gluw_ref[...]) + glub_ref[...])
    y = y * _silu(gate)
    xo_ref[...] = x + _rms(_mm(y, wout_ref[...]), gpost_ref[...])


def _s5_out(y, x, gpre, gpost, w, *, rows):
    r, d = x.shape
    e = D_INNER
    row = lambda i: (i, 0)
    return pl.pallas_call(
        _s5_out_kernel,
        grid=(r // rows,),
        in_specs=[pl.BlockSpec((rows, e), row), pl.BlockSpec((rows, d), row), _const((1, d)),
                  _const((d, e)), _const((e, e)), _const((1, e)), _const((e, d)), _const((1, d))],
        out_specs=pl.BlockSpec((rows, d), row),
        out_shape=jax.ShapeDtypeStruct((r, d), F32),
        compiler_params=_params("parallel"),
        name="s5_out",
    )(y, x, gpre, w["w_gate"], w["glu_w"], w["glu_b"], w["w_out"], gpost)


def _rwkv_proj_kernel(x_ref, xprev_ref, gpre_ref, mu_ref, wr_ref, wk_ref, wv_ref, wg_ref,
                      w0_ref, w1_ref, w2_ref, a0_ref, a1_ref, a2_ref,
                      r_ref, k_ref, v_ref, g_ref, dec_ref, a_ref, shift_ref,
                      prev_ref, *, nb, tt):
    rows = nb * tt

    @pl.when(pl.program_id(0) == 0)
    def _():
        prev_ref[...] = xprev_ref[...]

    xn = _rms(x_ref[...], gpre_ref[...])
    if tt > 1:
        shifted = jnp.concatenate([prev_ref[...], xn[:rows - nb]], axis=0)
    else:
        shifted = prev_ref[...]
    prev_ref[...] = xn[rows - nb:]
    shift_ref[...] = xn[rows - nb:]
    xx = shifted - xn
    mix = lambda n: xn + xx * mu_ref[n:n + 1, :]
    r_ref[...] = _mm(mix(0), wr_ref[...])
    k_ref[...] = _mm(mix(2), wk_ref[...])
    v_ref[...] = _mm(mix(3), wv_ref[...])
    g_ref[...] = _silu(_mm(mix(5), wg_ref[...]))
    w_raw = w0_ref[...] + _mm(jnp.tanh(_mm(mix(1), w1_ref[...])), w2_ref[...])
    dec_ref[...] = jnp.exp(-(jax.nn.sigmoid(w_raw) * math.exp(-0.5)))
    a_ref[...] = jax.nn.sigmoid(a0_ref[...] + _mm(_mm(mix(4), a1_ref[...]), a2_ref[...]))


def _rwkv_proj(x, x_prev, gpre, w, *, nb, tt):
    r, d = x.shape
    e = D_INNER
    rows = nb * tt
    lora = w["w1"].shape[1]
    row = lambda i: (i, 0)
    big = pl.BlockSpec((rows, e), row)
    return pl.pallas_call(
        functools.partial(_rwkv_proj_kernel, nb=nb, tt=tt),
        grid=(r // rows,),
        in_specs=[pl.BlockSpec((rows, d), row), _const((nb, d)), _const((1, d)), _const((6, d)),
                  _const((d, e)), _const((d, e)), _const((d, e)), _const((d, e)),
                  _const((1, e)), _const((d, lora)), _const((lora, e)),
                  _const((1, e)), _const((d, lora)), _const((lora, e))],
        out_specs=[big, big, big, big, big, big, _const_out((nb, d))],
        out_shape=[jax.ShapeDtypeStruct((r, e), F32)] * 6 + [jax.ShapeDtypeStruct((nb, d), F32)],
        scratch_shapes=[pltpu.VMEM((nb, d), F32)],
        compiler_params=_params("arbitrary"),
        name="rwkv_proj",
    )(x, x_prev, gpre, w["mu"], w["w_r"], w["w_k"], w["w_v"], w["w_g"],
      w["w0"], w["w1"], w["w2"], w["a0"], w["a1"], w["a2"])


RWKV_VSUB = 4
RWKV_VGRP = RWKV_HEAD // RWKV_VSUB
RWKV_BG = 8
RWKV_VBLK = 8
RWKV_UNROLL = 32


def _seg_sum(x):
    lane_axis = x.ndim - 1
    x = x + pltpu.roll(x, 2 * RWKV_HEADS, lane_axis)
    return x + pltpu.roll(x, RWKV_HEADS, lane_axis)


def _rwkv_rec_kernel(r_ref, w_ref, k_ref, v_ref, a_ref, vprev_ref, kk_ref, ka_ref, rk_ref,
                     kkrep_ref, karep_ref, lnw_ref, lnb_ref, s0_ref, o_ref, so_ref,
                     s_ref, vec_ref, sa_scale_ref, bon_ref, yraw_ref, *, tt):
    n = RWKV_HEAD
    nh = RWKV_HEADS
    shape = (RWKV_BG, LANES)
    step = pl.program_id(1)
    last = pl.num_programs(1) - 1

    @pl.when(step == 0)
    def _():
        s_ref[...] = s0_ref[0]
        yraw_ref[...] = jnp.zeros_like(yraw_ref)
        bon_ref[...] = jnp.zeros_like(bon_ref)

    tiles = [slice(j * LANES, (j + 1) * LANES) for j in range(RWKV_VGRP)]

    bon_prev = bon_ref[...]
    tot = yraw_ref[:, :, tiles[0]]
    for sl in tiles[1:]:
        tot = tot + yraw_ref[:, :, sl]
    mean = _seg_sum(tot) * (1.0 / n)
    sq = jnp.zeros((tt,) + shape, F32)
    for sl in tiles:
        yc = yraw_ref[:, :, sl] - mean
        sq = sq + yc * yc
    rstd = lax.rsqrt(_seg_sum(sq) * (1.0 / n) + RWKV_LN_EPS)
    for sl in tiles:
        o_ref[:, :, sl] = ((yraw_ref[:, :, sl] - mean) * rstd * lnw_ref[:, sl] + lnb_ref[:, sl]
                           + bon_prev * vprev_ref[:, :, sl])

    seg = lax.broadcasted_iota(jnp.int32, (tt,) + shape, 2) // nh
    seg_low = seg < 2
    seg_odd = (seg & 1) == 1

    def rep4(x):
        rolled = [x] + [pltpu.roll(x, q * nh, 2) for q in range(1, RWKV_VSUB)]
        pair = [jnp.where(seg_low, rolled[i], rolled[(i + 2) % RWKV_VSUB]) for i in range(RWKV_VSUB)]
        return [jnp.where(seg_odd, pair[(1 - s_) % RWKV_VSUB], pair[(-s_) % RWKV_VSUB])
                for s_ in range(RWKV_VSUB)]

    n2 = jnp.zeros((tt,) + shape, F32)
    bon = jnp.zeros((tt,) + shape, F32)
    for j, sl in enumerate(tiles):
        r = r_ref[:, :, sl]
        k = k_ref[:, :, sl]
        a = a_ref[:, :, sl]
        kk = k * kk_ref[:, sl]
        n2 = n2 + kk * kk
        bon = bon + r * (k * (1.0 + (a - 1.0) * ka_ref[:, sl])) * rk_ref[:, sl]
        r_rep = rep4(r)
        w_rep = rep4(w_ref[:, :, sl])
        k_rep = rep4(k)
        a_rep = rep4(a)
        for s_ in range(RWKV_VSUB):
            c = RWKV_VSUB * j + s_
            kk_c = k_rep[s_] * kkrep_ref[c]
            vec_ref[:, 0, c] = r_rep[s_]
            vec_ref[:, 1, c] = w_rep[s_]
            vec_ref[:, 2, c] = kk_c
            vec_ref[:, 3, c] = kk_c * a_rep[s_]
            vec_ref[:, 4, c] = k_rep[s_] * (1.0 + (a_rep[s_] - 1.0) * karep_ref[c])
    inv = lax.rsqrt(jnp.maximum(_seg_sum(n2), 1e-24))
    sa_scale_ref[...] = -(inv * inv)
    bon_ref[...] = _seg_sum(bon)

    def token(t, carry):
        sa_scale = sa_scale_ref[t]
        zero = jnp.zeros((RWKV_VBLK,) + shape, F32)
        for vb in range(RWKV_VGRP // RWKV_VBLK):
            vsl = slice(vb * RWKV_VBLK, (vb + 1) * RWKV_VBLK)
            vv = jnp.stack([v_ref[t, :, sl] for sl in tiles[vsl]])

            def sa_step(c, acc):
                return acc + s_ref[c, vsl] * vec_ref[t, 2, c]

            sa = lax.fori_loop(0, n, sa_step, zero, unroll=RWKV_UNROLL) * sa_scale

            def update_step(c, acc):
                s_new = s_ref[c, vsl] * vec_ref[t, 1, c] + sa * vec_ref[t, 3, c] + vv * vec_ref[t, 4, c]
                s_ref[c, vsl] = s_new
                return acc + s_new * vec_ref[t, 0, c]

            y = lax.fori_loop(0, n, update_step, zero, unroll=RWKV_UNROLL)
            for i, sl in enumerate(tiles[vsl]):
                yraw_ref[t, :, sl] = y[i]
        return carry

    @pl.when(step < last)
    def _():
        lax.fori_loop(0, tt, token, 0)

    @pl.when(step == last)
    def _():
        so_ref[0] = s_ref[...]


def _rwkv_rec(r, w, k, v, a, pk, s0, *, tt):
    t_len, nb, e = r.shape
    n = RWKV_HEAD
    n_tiles = t_len // tt
    cur = pl.BlockSpec((tt, RWKV_BG, e), lambda bi, si: (jnp.minimum(si, n_tiles - 1), bi, 0))
    prev = pl.BlockSpec((tt, RWKV_BG, e), lambda bi, si: (jnp.maximum(si - 1, 0), bi, 0))
    st = pl.BlockSpec((1, n, RWKV_VGRP, RWKV_BG, LANES), lambda bi, si: (bi, 0, 0, 0, 0))
    return pl.pallas_call(
        functools.partial(_rwkv_rec_kernel, tt=tt),
        grid=(nb // RWKV_BG, n_tiles + 1),
        in_specs=[cur] * 5 + [prev] + [_const((RWKV_BG, e))] * 3
                 + [_const((n, RWKV_BG, LANES))] * 2 + [_const((RWKV_BG, e))] * 2 + [st],
        out_specs=[prev, st],
        out_shape=[jax.ShapeDtypeStruct((t_len, nb, e), F32), jax.ShapeDtypeStruct(s0.shape, F32)],
        scratch_shapes=[pltpu.VMEM((n, RWKV_VGRP, RWKV_BG, LANES), F32),
                        pltpu.VMEM((tt, 5, n, RWKV_BG, LANES), F32),
                        pltpu.VMEM((tt, RWKV_BG, LANES), F32),
                        pltpu.VMEM((tt, RWKV_BG, LANES), F32),
                        pltpu.VMEM((tt, RWKV_BG, e), F32)],
        compiler_params=_params("parallel", "arbitrary"),
        name="rwkv_recurrence",
    )(r, w, k, v, a, v, pk["k_k"], pk["k_a"], pk["r_k"], pk["k_k_rep"], pk["k_a_rep"],
      pk["ln_w"], pk["ln_b"], s0)


def _gated_out_kernel(y_ref, g_ref, x_ref, w_ref, gpost_ref, xo_ref):
    y = y_ref[...] * g_ref[...]
    xo_ref[...] = x_ref[...] + _rms(_mm(y, w_ref[...]), gpost_ref[...])


def _gated_out(y, g, x, w, gpost, *, rows):
    r, d = x.shape
    e = D_INNER
    row = lambda i: (i, 0)
    return pl.pallas_call(
        _gated_out_kernel,
        grid=(r // rows,),
        in_specs=[pl.BlockSpec((rows, e), row), pl.BlockSpec((rows, e), row),
                  pl.BlockSpec((rows, d), row), _const((e, d)), _const((1, d))],
        out_specs=pl.BlockSpec((rows, d), row),
        out_shape=jax.ShapeDtypeStruct((r, d), F32),
        compiler_params=_params("parallel"),
        name="gated_out",
    )(y, g, x, w, gpost)


def _ret_proj_kernel(x_ref, gpre_ref, wq_ref, wk_ref, wv_ref, wg_ref, q_ref, k_ref, v_ref, g_ref):
    xn = _rms(x_ref[...], gpre_ref[...]).astype(BF16)
    q_ref[...] = jnp.dot(xn, wq_ref[...], preferred_element_type=F32)
    k_ref[...] = jnp.dot(xn, wk_ref[...], preferred_element_type=F32)
    v_ref[...] = jnp.dot(xn, wv_ref[...], preferred_element_type=F32)
    g_ref[...] = _silu(jnp.dot(xn, wg_ref[...], preferred_element_type=F32))


def _ret_proj(x, gpre, w, *, rows):
    r, d = x.shape
    e = D_INNER
    row = lambda i: (i, 0)
    return pl.pallas_call(
        _ret_proj_kernel,
        grid=(r // rows,),
        in_specs=[pl.BlockSpec((rows, d), row), _const((1, d)), _const((d, d)), _const((d, d)),
                  _const((d, e)), _const((d, e))],
        out_specs=[pl.BlockSpec((rows, d), row), pl.BlockSpec((rows, d), row),
                   pl.BlockSpec((rows, e), row), pl.BlockSpec((rows, e), row)],
        out_shape=[jax.ShapeDtypeStruct((r, d), F32), jax.ShapeDtypeStruct((r, d), F32),
                   jax.ShapeDtypeStruct((r, e), F32), jax.ShapeDtypeStruct((r, e), F32)],
        compiler_params=_params("parallel"),
        name="ret_proj",
    )(x, gpre, w["w_q"], w["w_k"], w["w_v"], w["w_g"])


def _ret_core_kernel(q_ref, kt_ref, v_ref, cos_ref, sin_ref, cost_ref, sint_ref, mask_ref,
                     qdec_ref, kdec_ref, gl_ref, s0_ref, *rest, fuse_out):
    half = RET_DK // 2
    if fuse_out:
        g_ref, x_ref, wo_ref, gpost_ref, out_ref, so_ref, s_ref = rest
    else:
        out_ref, so_ref, s_ref = rest

    @pl.when(pl.program_id(1) == 0)
    def _():
        s_ref[...] = s0_ref[0]

    cos = cos_ref[...]
    sin = sin_ref[...]
    cost = cost_ref[...]
    sint = sint_ref[...]
    proj = None
    for h in range(RET_HEADS):
        q1 = q_ref[:, h * RET_DK:h * RET_DK + half]
        q2 = q_ref[:, h * RET_DK + half:(h + 1) * RET_DK]
        qr = jnp.concatenate([q1 * cos - q2 * sin, q2 * cos + q1 * sin], axis=1)
        k1 = kt_ref[0, h * RET_DK:h * RET_DK + half, :]
        k2 = kt_ref[0, h * RET_DK + half:(h + 1) * RET_DK, :]
        krt = jnp.concatenate([k1 * cost - k2 * sint, k2 * cost + k1 * sint], axis=0)
        krt = krt * (RET_DK ** -0.5)
        vh = v_ref[:, h * RET_DV:(h + 1) * RET_DV].astype(BF16)
        qb = qr.astype(BF16)
        scores = jnp.dot(qb, krt.astype(BF16), preferred_element_type=F32) * mask_ref[h]
        s_old = s_ref[h]
        y = jnp.dot(scores.astype(BF16), vh, preferred_element_type=F32)
        y = y + jnp.dot(qb, s_old.astype(BF16), preferred_element_type=F32) * qdec_ref[h]
        kw = (krt * kdec_ref[h]).astype(BF16)
        s_ref[h] = gl_ref[h] * s_old + jnp.dot(kw, vh, preferred_element_type=F32)
        y = y * lax.rsqrt(jnp.mean(y * y, axis=-1, keepdims=True) + NORM_EPS)
        if fuse_out:
            yg = y * g_ref[:, h * RET_DV:(h + 1) * RET_DV]
            part = _mm(yg, wo_ref[h * RET_DV:(h + 1) * RET_DV, :])
            proj = part if proj is None else proj + part
        else:
            out_ref[:, h * RET_DV:(h + 1) * RET_DV] = y
    if fuse_out:
        out_ref[...] = x_ref[...] + _rms(proj, gpost_ref[...])

    @pl.when(pl.program_id(1) == pl.num_programs(1) - 1)
    def _():
        so_ref[0] = s_ref[...]


def _ret_core(q, kt, v, tabs, s0, *, lc, out=None, drop_first=False):
    b, dkh, t_len = kt.shape
    nc = t_len // lc
    e = D_INNER
    d = D_MODEL
    rowc = lambda bi, ci: (bi * nc + ci, 0)
    state = pl.BlockSpec((1, RET_HEADS, RET_DK, RET_DV), lambda bi, ci: (bi, 0, 0, 0))
    half = RET_DK // 2
    in_specs = [pl.BlockSpec((lc, dkh), rowc),
                pl.BlockSpec((1, dkh, lc), lambda bi, ci: (bi, 0, ci)),
                pl.BlockSpec((lc, e), rowc),
                pl.BlockSpec((lc, half), lambda bi, ci: (ci, 0)),
                pl.BlockSpec((lc, half), lambda bi, ci: (ci, 0)),
                pl.BlockSpec((half, lc), lambda bi, ci: (0, ci)),
                pl.BlockSpec((half, lc), lambda bi, ci: (0, ci)),
                _const((RET_HEADS, lc, lc)), _const((RET_HEADS, lc, 1)),
                _const((RET_HEADS, 1, lc)), _const((RET_HEADS, 1, 1)), state]
    args = [q, kt, v, tabs["cos"], tabs["sin"], tabs["cos_t"], tabs["sin_t"], tabs["mask"],
            tabs["qdec"], tabs["kdec"], tabs["gl"], s0]
    width = e
    if out is not None:
        in_specs += [pl.BlockSpec((lc, e), rowc), pl.BlockSpec((lc, d), rowc), _const((e, d)),
                     _const((1, d))]
        args += list(out)
        width = d
    out_rows, out_map = b * t_len, rowc
    if drop_first:
        out_rows = b * (t_len - lc)
        out_map = lambda bi, ci: (bi * (nc - 1) + jnp.maximum(ci - 1, 0), 0)
    return pl.pallas_call(
        functools.partial(_ret_core_kernel, fuse_out=out is not None),
        grid=(b, nc),
        in_specs=in_specs,
        out_specs=[pl.BlockSpec((lc, width), out_map), state],
        out_shape=[jax.ShapeDtypeStruct((out_rows, width), F32),
                   jax.ShapeDtypeStruct((b, RET_HEADS, RET_DK, RET_DV), F32)],
        scratch_shapes=[pltpu.VMEM((RET_HEADS, RET_DK, RET_DV), F32)],
        compiler_params=_params("parallel", "arbitrary"),
        name="ret_core",
    )(*args)


def _ret_tables(pos, lc):
    half = RET_DK // 2
    inv = ROPE_BASE ** (-jnp.arange(half, dtype=F32) / half)
    ang = pos.astype(F32)[:, None] * inv[None, :]
    cos = jnp.cos(ang)
    sin = jnp.sin(ang)
    log_g = jnp.log1p(-jnp.exp2(-5.0 - jnp.arange(RET_HEADS, dtype=F32)))
    n = jnp.arange(lc, dtype=F32)
    diff = n[:, None] - n[None, :]
    mask = jnp.where(diff[None] >= 0, jnp.exp(diff[None] * log_g[:, None, None]), 0.0)
    qdec = jnp.exp((n[None, :] + 1.0) * log_g[:, None])[:, :, None]
    kdec = jnp.exp((lc - 1.0 - n)[None, :] * log_g[:, None])[:, None, :]
    gl = jnp.exp(lc * log_g)[:, None, None]
    return dict(cos=cos, sin=sin, cos_t=cos.T, sin_t=sin.T, mask=mask, qdec=qdec, kdec=kdec, gl=gl)


def _s5_block_weights(bb_re, bb_im, c_re, c_im):
    gl = S5_GROUPS // S5_KB
    eye = jnp.eye(gl, dtype=BF16)

    def bdiag_b(bb):
        x = bb.astype(BF16).reshape(S5_KB, gl, S5_GROUP, S5_STATE)
        x = x[:, :, :, None, :] * eye[None, :, None, :, None]
        return x.reshape(S5_KB, gl * S5_GROUP, gl * S5_STATE)

    def bdiag_c(c):
        x = jnp.swapaxes(c, 1, 2).astype(BF16).reshape(S5_KB, gl, S5_STATE, S5_GROUP)
        x = x[:, :, :, None, :] * eye[None, :, None, :, None]
        return x.reshape(S5_KB, gl * S5_STATE, gl * S5_GROUP)

    bw = jnp.concatenate([bdiag_b(bb_re), bdiag_b(bb_im)], axis=2)
    cw = jnp.concatenate([bdiag_c(c_re), -bdiag_c(c_im)], axis=1)
    return bw, cw


def _head_minor(x):
    lead = x.shape[:-1]
    x = x.reshape(lead + (RWKV_HEADS, RWKV_HEAD))
    return jnp.swapaxes(x, -1, -2).reshape(lead + (D_INNER,))


def _key_param(p):
    return jnp.broadcast_to(p.reshape(1, D_INNER), (RWKV_BG, D_INNER))


def _key_param_rep(p):
    x = jnp.tile(p.reshape(RWKV_HEAD, 1, RWKV_HEADS), (1, 1, RWKV_VSUB))
    return jnp.broadcast_to(x, (RWKV_HEAD, RWKV_BG, LANES))


def _wkv_to_blocks(s, nb):
    x = s.reshape(nb // RWKV_BG, RWKV_BG, RWKV_HEADS, RWKV_VGRP, RWKV_VSUB, RWKV_HEAD)
    x = jnp.transpose(x, (0, 5, 3, 1, 4, 2))
    return x.reshape(nb // RWKV_BG, RWKV_HEAD, RWKV_VGRP, RWKV_BG, LANES)


def _wkv_from_blocks(x, nb):
    x = x.reshape(nb // RWKV_BG, RWKV_HEAD, RWKV_VGRP, RWKV_BG, RWKV_VSUB, RWKV_HEADS)
    x = jnp.transpose(x, (0, 3, 5, 2, 4, 1))
    return x.reshape(nb, RWKV_HEADS, RWKV_HEAD, RWKV_HEAD)


def _run_stream(x, states, wts, norm_pre, norm_post, *, nb, t_len, pos0, tiles, n_drop):
    d = D_MODEL
    e = D_INNER
    new = {}
    gp = lambda i: (norm_pre[i].reshape(1, d), norm_post[i].reshape(1, d))

    gpre, gpost = gp(0)
    cs = jnp.transpose(states["lru_conv"], (1, 0, 2)).reshape((LRU_CONV_W - 1) * nb, e)
    x, cso, ho = _lru_layer(x, cs, states["lru_h"], gpre, gpost, wts["lru"], nb=nb, tt=tiles["lru_tt"])
    new["lru_conv"] = jnp.transpose(cso.reshape(LRU_CONV_W - 1, nb, e), (1, 0, 2))
    new["lru_h"] = ho

    gpre, gpost = gp(1)
    w = wts["s5"]
    gn = S5_GROUPS * S5_STATE
    y, sre, sim = _s5_core(x, gpre, w, states["s5_re"].reshape(nb, gn),
                           states["s5_im"].reshape(nb, gn), nb=nb, tt=tiles["s5_tt"])
    x = _s5_out(y, x, gpre, gpost, w, rows=tiles["rows"])
    new["s5_re"] = sre.reshape(nb, S5_GROUPS, S5_STATE)
    new["s5_im"] = sim.reshape(nb, S5_GROUPS, S5_STATE)

    gpre, gpost = gp(2)
    w = wts["rwkv"]
    r, k, v, g, dec, a, shift = _rwkv_proj(x, states["rwkv_shift"], gpre, w, nb=nb, tt=tiles["rwkv_tt"])
    as3 = lambda z: z.reshape(t_len, nb, e)
    pk = dict(k_k=_key_param(w["k_k"]), k_a=_key_param(w["k_a"]), r_k=_key_param(w["r_k"]),
              k_k_rep=_key_param_rep(w["k_k"]), k_a_rep=_key_param_rep(w["k_a"]),
              ln_w=_key_param(w["ln_w"]), ln_b=_key_param(w["ln_b"]))
    o, s_new = _rwkv_rec(as3(r), as3(dec), as3(k), as3(v), as3(a), pk,
                         _wkv_to_blocks(states["rwkv_wkv"], nb), tt=tiles["rec_tt"])
    x = _gated_out(o.reshape(t_len * nb, e), g, x, w["w_o"], gpost, rows=tiles["rows"])
    new["rwkv_shift"] = shift
    new["rwkv_wkv"] = _wkv_from_blocks(s_new, nb)

    gpre, gpost = gp(3)
    w = wts["ret"]
    lc = tiles["ret_lc"]
    t_pad = -(-t_len // lc) * lc
    pad = t_pad - t_len
    xb = jnp.transpose(x.reshape(t_len, nb, d), (1, 0, 2))
    xb = jnp.pad(xb, ((0, 0), (pad, 0), (0, 0))).reshape(nb * t_pad, d)
    q, kk, v, g = _ret_proj(xb, gpre, w, rows=tiles["ret_rows"])
    kt = jnp.transpose(kk.reshape(nb, t_pad, RET_HEADS * RET_DK), (0, 2, 1))
    tabs = _ret_tables(pos0 - pad + jnp.arange(t_pad, dtype=jnp.int32), lc)
    if lc % LANES == 0:
        drop_first = pad + n_drop == lc
        xb, s_new = _ret_core(q, kt, v, tabs, states["ret"], lc=lc, out=(g, xb, w["w_o"], gpost),
                              drop_first=drop_first)
        xb = xb.reshape(nb, -1, d)
        if not drop_first:
            xb = xb[:, pad + n_drop:]
    else:
        y, s_new = _ret_core(q, kt, v, tabs, states["ret"], lc=lc)
        xb = _gated_out(y, g, xb, w["w_o"], gpost, rows=tiles["ret_rows"])
        xb = xb.reshape(nb, t_pad, d)[:, pad + n_drop:]
    new["ret"] = s_new
    return xb, new


def kernel(x_prompt, x_sample, state_lru_conv, state_lru_h, state_s5_re, state_s5_im, state_rwkv_shift, state_rwkv_wkv, state_ret, meta_tokens, norm_pre, norm_post, lru_w_in, lru_conv_w, lru_conv_b, lru_wa, lru_ba, lru_wx, lru_bx, lru_lam, lru_w_out, s5_w_in, s5_log_dt, s5_a_re, s5_a_im, s5_b_re, s5_b_im, s5_c_re, s5_c_im, s5_d, s5_glu_w, s5_glu_b, s5_w_out, rwkv_mu, rwkv_w_r, rwkv_w_k, rwkv_w_v, rwkv_w_g, rwkv_w0, rwkv_w1, rwkv_w2, rwkv_a0, rwkv_a1, rwkv_a2, rwkv_k_k, rwkv_k_a, rwkv_r_k, rwkv_ln_w, rwkv_ln_b, rwkv_w_o, ret_w_q, ret_w_k, ret_w_v, ret_w_g, ret_w_o):
    d = D_MODEL
    e = D_INNER
    bp, seq, _ = x_prompt.shape
    bs, dec_seq, _ = x_sample.shape
    t_p = seq + N_META
    bf = lambda x: x.astype(BF16)
    vec = lambda x: x.reshape(1, -1)
    hm = _head_minor

    ab_re, ab_im, bb_re, bb_im = _s5_discretize(
        s5_log_dt[0], s5_a_re[0], s5_a_im[0],
        jnp.swapaxes(s5_b_re[0], 1, 2), jnp.swapaxes(s5_b_im[0], 1, 2))
    bw, cw = _s5_block_weights(bb_re, bb_im, s5_c_re[0], s5_c_im[0])
    wts = dict(
        lru=dict(w_in=bf(lru_w_in[0]), conv_w=lru_conv_w[0], conv_b=vec(lru_conv_b[0]),
                 wa=bf(lru_wa[0]), ba=vec(lru_ba[0]), wx=bf(lru_wx[0]), bx=vec(lru_bx[0]),
                 lam=vec(lru_lam[0]), w_out=bf(lru_w_out[0])),
        s5=dict(w_u=bf(s5_w_in[0, :, :e]), w_gate=bf(s5_w_in[0, :, e:]), bw=bw, cw=cw, ab_re=vec(ab_re), ab_im=vec(ab_im),
                d=vec(s5_d[0]), glu_w=bf(s5_glu_w[0]), glu_b=vec(s5_glu_b[0]), w_out=bf(s5_w_out[0])),
        rwkv=dict(mu=rwkv_mu[0], w_r=bf(hm(rwkv_w_r[0])), w_k=bf(hm(rwkv_w_k[0])),
                  w_v=bf(hm(rwkv_w_v[0])), w_g=bf(hm(rwkv_w_g[0])), w0=vec(hm(rwkv_w0[0])),
                  w1=bf(rwkv_w1[0]), w2=bf(hm(rwkv_w2[0])), a0=vec(hm(rwkv_a0[0])),
                  a1=bf(rwkv_a1[0]), a2=bf(hm(rwkv_a2[0])),
                  k_k=hm(rwkv_k_k[0]), k_a=hm(rwkv_k_a[0]), r_k=hm(rwkv_r_k[0]),
                  ln_w=hm(rwkv_ln_w[0]), ln_b=hm(rwkv_ln_b[0]), w_o=bf(hm(rwkv_w_o[0].T).T)),
        ret=dict(w_q=bf(ret_w_q[0]), w_k=bf(ret_w_k[0]), w_v=bf(ret_w_v[0]), w_g=bf(ret_w_g[0]),
                 w_o=bf(ret_w_o[0])),
    )

    meta = jnp.broadcast_to(meta_tokens[None].astype(x_prompt.dtype), (bp, N_META, d))
    hp = jnp.concatenate([meta, x_prompt], axis=1)
    z = lambda *s: jnp.zeros(s, F32)
    p_states = dict(lru_conv=z(bp, LRU_CONV_W - 1, e), lru_h=z(bp, e),
                    s5_re=z(bp, S5_GROUPS, S5_STATE), s5_im=z(bp, S5_GROUPS, S5_STATE),
                    rwkv_shift=z(bp, d), rwkv_wkv=z(bp, RWKV_HEADS, RWKV_HEAD, RWKV_HEAD),
                    ret=z(bp, RET_HEADS, RET_DK, RET_DV))
    p_tiles = dict(lru_tt=_divisor_tile(t_p, 48, 1), s5_tt=_divisor_tile(t_p, 48, 1),
                   rwkv_tt=_divisor_tile(t_p, 24, 1), rec_tt=_divisor_tile(t_p, 16, 1),
                   rows=_divisor_tile(t_p * bp, 384, 8), ret_lc=RET_CHUNK,
                   ret_rows=_divisor_tile(bp * (-(-t_p // RET_CHUNK) * RET_CHUNK), 512, 8))
    yp, new_p = _run_stream(hp, p_states, wts, norm_pre, norm_post, nb=bp, t_len=t_p, pos0=0,
                            tiles=p_tiles, n_drop=N_META)

    s_states = dict(lru_conv=state_lru_conv[0], lru_h=state_lru_h[0], s5_re=state_s5_re[0],
                    s5_im=state_s5_im[0], rwkv_shift=state_rwkv_shift[0],
                    rwkv_wkv=state_rwkv_wkv[0], ret=state_ret[0])
    s_rows = _divisor_tile(dec_seq * bs, 256, 8)
    s_tiles = dict(lru_tt=2, s5_tt=1, rwkv_tt=2, rec_tt=dec_seq, rows=s_rows, ret_lc=dec_seq,
                   ret_rows=s_rows)
    ys, new_s = _run_stream(x_sample, s_states, wts, norm_pre, norm_post, nb=bs, t_len=dec_seq,
                            pos0=PAST_LEN, tiles=s_tiles, n_drop=0)

    names = ("lru_conv", "lru_h", "s5_re", "s5_im", "rwkv_shift", "rwkv_wkv", "ret")
    return ((yp, ys)
            + tuple(new_p[n][None] for n in names)
            + tuple(new_s[n][None] for n in names))
```

```python
import functools
import math

import jax
import jax.numpy as jnp
from jax import lax
from jax.experimental import pallas as pl
from jax.experimental.pallas import tpu as pltpu

F32 = jnp.float32
BF16 = jnp.bfloat16

D_MODEL = 1024
D_INNER = 2048
N_META = 16
NORM_EPS = 1e-6
LRU_CONV_W = 4
LRU_BLOCKS = 16
LRU_BLOCK = 128
LRU_C = 8.0
S5_GROUP = 16
S5_GROUPS = 128
S5_STATE = 64
S5_KB = 8
RWKV_HEAD = 64
RWKV_HEADS = 32
RWKV_LN_EPS = 64e-5
RET_HEADS = 4
RET_DK = 256
RET_DV = 512
ROPE_BASE = 10000.0
PAST_LEN = 16384
RET_CHUNK = 128

LANES = 128
VMEM_LIMIT = 56 * 1024 * 1024


def _params(*sem):
    return pltpu.CompilerParams(dimension_semantics=sem, vmem_limit_bytes=VMEM_LIMIT)


def _const(shape):
    zeros = (0,) * len(shape)
    return pl.BlockSpec(shape, lambda *_: zeros, pipeline_mode=pl.Buffered(1))


def _const_out(shape):
    zeros = (0,) * len(shape)
    return pl.BlockSpec(shape, lambda *_: zeros)


def _divisor_tile(n, pref, mult):
    best = mult
    t = mult
    while t <= min(n, pref):
        if n % t == 0:
            best = t
        t += mult
    assert n % best == 0
    return best


def _rms(x, g):
    return x * lax.rsqrt(jnp.mean(x * x, axis=-1, keepdims=True) + NORM_EPS) * g


def _mm(a, w):
    return jnp.dot(a.astype(BF16), w, preferred_element_type=F32)


def _silu(x):
    return x * jax.nn.sigmoid(x)


def _softplus(x):
    return jnp.maximum(x, 0.0) + jnp.log1p(jnp.exp(-jnp.abs(x)))


def _lru_kernel(x_ref, cs_ref, h0_ref, gpre_ref, gpost_ref, win_ref, cw_ref, cb_ref,
                wa_ref, ba_ref, wx_ref, bx_ref, lam_ref, wout_ref,
                xo_ref, cso_ref, ho_ref,
                uz_ref, b_ref, tail_ref, h_ref, *, nb, tt):
    rows = nb * tt
    e = D_INNER

    @pl.when(pl.program_id(0) == 0)
    def _():
        tail_ref[...] = cs_ref[...]
        h_ref[...] = h0_ref[...]

    d = D_MODEL
    x = jnp.concatenate([x_ref[:, t * d:(t + 1) * d] for t in range(tt)], axis=0)
    xn = _rms(x, gpre_ref[...])
    uz_ref[...] = _mm(xn, win_ref[...])

    for j in range(LRU_BLOCKS):
        sl = slice(j * LRU_BLOCK, (j + 1) * LRU_BLOCK)
        ext = jnp.concatenate([tail_ref[:, sl], uz_ref[:, sl]], axis=0)
        cw = cw_ref[:, sl]
        xc = cb_ref[:, sl]
        for jj in range(LRU_CONV_W):
            xc = xc + ext[jj * nb:jj * nb + rows] * cw[jj:jj + 1]
        tail_ref[:, sl] = ext[rows:rows + (LRU_CONV_W - 1) * nb]
        gate_r = jax.nn.sigmoid(_mm(xc, wa_ref[j]) + ba_ref[:, sl])
        gate_i = jax.nn.sigmoid(_mm(xc, wx_ref[j]) + bx_ref[:, sl])
        log_a = -LRU_C * gate_r * _softplus(-lam_ref[:, sl])
        a = jnp.exp(log_a)
        bx = jnp.sqrt(-jnp.tanh(log_a) * (a * a + 1.0)) * gate_i * xc
        h = h_ref[:, sl]
        hs = []
        for t in range(tt):
            h = a[t * nb:(t + 1) * nb] * h + bx[t * nb:(t + 1) * nb]
            hs.append(h)
        h_ref[:, sl] = h
        b_ref[:, sl] = jnp.concatenate(hs, axis=0)

    y = b_ref[...] * _silu(uz_ref[:, e:])
    out = _mm(y, wout_ref[...])
    xo_ref[...] = x + _rms(out, gpost_ref[...])
    cso_ref[...] = tail_ref[...]
    ho_ref[...] = h_ref[...]


def _lru_layer(x, conv_state, h0, gpre, gpost, w, *, nb, tt):
    _, t_len, d = x.shape
    r = nb * t_len
    e = D_INNER
    rows = nb * tt
    ctail = (LRU_CONV_W - 1) * nb
    row = lambda i: (i, 0)
    return pl.pallas_call(
        functools.partial(_lru_kernel, nb=nb, tt=tt),
        grid=(r // rows,),
        in_specs=[
            pl.BlockSpec((nb, tt * d), lambda i: (0, i)), _const((ctail, e)), _const((nb, e)),
            _const((1, d)), _const((1, d)), _const((d, 2 * e)),
            _const((LRU_CONV_W, e)), _const((1, e)),
            _const((LRU_BLOCKS, LRU_BLOCK, LRU_BLOCK)), _const((1, e)),
            _const((LRU_BLOCKS, LRU_BLOCK, LRU_BLOCK)), _const((1, e)),
            _const((1, e)), _const((e, d)),
        ],
        out_specs=[pl.BlockSpec((rows, d), row), _const_out((ctail, e)), _const_out((nb, e))],
        out_shape=[jax.ShapeDtypeStruct((r, d), F32),
                   jax.ShapeDtypeStruct((ctail, e), F32),
                   jax.ShapeDtypeStruct((nb, e), F32)],
        scratch_shapes=[pltpu.VMEM((rows, 2 * e), F32), pltpu.VMEM((rows, e), F32),
                        pltpu.VMEM((ctail, e), F32),
                        pltpu.VMEM((nb, e), F32)],
        compiler_params=_params("arbitrary"),
        name="lru_layer",
    )(x.reshape(nb, t_len * d), conv_state, h0, gpre, gpost, w["w_in"], w["conv_w"], w["conv_b"],
      w["wa"], w["ba"], w["wx"], w["bx"], w["lam"], w["w_out"])


def _s5_disc_kernel(logdt_ref, are_ref, aim_ref, bre_ref, bim_ref,
                    abre_ref, abim_ref, bbre_ref, bbim_ref):
    dt = jnp.exp(logdt_ref[...])
    a_re = are_ref[...]
    a_im = aim_ref[...]
    mag = jnp.exp(dt * a_re)
    ang = dt * a_im
    ab_re = mag * jnp.cos(ang)
    ab_im = mag * jnp.sin(ang)
    den = a_re * a_re + a_im * a_im
    f_re = ((ab_re - 1.0) * a_re + ab_im * a_im) / den
    f_im = (ab_im * a_re - (ab_re - 1.0) * a_im) / den
    abre_ref[...] = ab_re
    abim_ref[...] = ab_im
    b_re = bre_ref[...]
    b_im = bim_ref[...]
    fr = f_re[:, None, :]
    fi = f_im[:, None, :]
    bbre_ref[...] = fr * b_re - fi * b_im
    bbim_ref[...] = fr * b_im + fi * b_re


def _s5_discretize(log_dt, a_re, a_im, b_re, b_im):
    g, n = a_re.shape
    c = b_re.shape[1]
    return pl.pallas_call(
        _s5_disc_kernel,
        out_shape=[jax.ShapeDtypeStruct((g, n), F32), jax.ShapeDtypeStruct((g, n), F32),
                   jax.ShapeDtypeStruct((g, c, n), F32), jax.ShapeDtypeStruct((g, c, n), F32)],
        name="s5_discretize",
    )(log_dt.reshape(g, 1), a_re, a_im, b_re, b_im)


def _gelu_tanh(x):
    return 0.5 * x * (1.0 + jnp.tanh(math.sqrt(2.0 / math.pi) * (x + 0.044715 * (x * x * x))))


def _s5_core_kernel(x_ref, gpre_ref, wu_ref, sre_ref, sim_ref, bw_ref, cw_ref, abre_ref, abim_ref,
                    d_ref, y_ref, sreo_ref, simo_ref,
                    xn_ref, bu_ref, xr_ref, xi_ref, *, nb, tt):
    gn = S5_GROUPS * S5_STATE // S5_KB
    ch = D_INNER // S5_KB

    @pl.when(pl.program_id(0) == 0)
    def _():
        xr_ref[...] = sre_ref[...]
        xi_ref[...] = sim_ref[...]

    xn_ref[...] = _rms(x_ref[...], gpre_ref[...]).astype(BF16)

    for kb in range(S5_KB):
        lanes = slice(kb * gn, (kb + 1) * gn)
        cols = slice(kb * ch, (kb + 1) * ch)
        u = jnp.dot(xn_ref[...], wu_ref[:, cols], preferred_element_type=F32)
        bu_ref[...] = _mm(u, bw_ref[kb])
        ar = abre_ref[:, lanes]
        ai = abim_ref[:, lanes]

        def step(t, carry):
            xr, xi = carry
            r = pl.ds(pl.multiple_of(t * nb, nb), nb)
            nr = ar * xr - ai * xi + bu_ref[r, :gn]
            ni = ar * xi + ai * xr + bu_ref[r, gn:]
            bu_ref[r, :gn] = nr
            bu_ref[r, gn:] = ni
            return nr, ni

        xr, xi = lax.fori_loop(0, tt, step, (xr_ref[:, lanes], xi_ref[:, lanes]), unroll=True)
        xr_ref[:, lanes] = xr
        xi_ref[:, lanes] = xi
        y = _mm(bu_ref[...], cw_ref[kb]) + d_ref[:, cols] * u
        y_ref[:, cols] = _gelu_tanh(y)

    sreo_ref[...] = xr_ref[...]
    simo_ref[...] = xi_ref[...]


def _s5_core(x, gpre, w, s_re, s_im, *, nb, tt):
    r, d = x.shape
    e = D_INNER
    gn = S5_GROUPS * S5_STATE
    rows = nb * tt
    row = lambda i: (i, 0)
    return pl.pallas_call(
        functools.partial(_s5_core_kernel, nb=nb, tt=tt),
        grid=(r // rows,),
        in_specs=[pl.BlockSpec((rows, d), row), _const((1, d)), _const((d, e)),
                  _const((nb, gn)), _const((nb, gn)), _const(w["bw"].shape), _const(w["cw"].shape),
                  _const((1, gn)), _const((1, gn)), _const((1, e))],
        out_specs=[pl.BlockSpec((rows, e), row), _const_out((nb, gn)), _const_out((nb, gn))],
        out_shape=[jax.ShapeDtypeStruct((r, e), F32), jax.ShapeDtypeStruct((nb, gn), F32),
                   jax.ShapeDtypeStruct((nb, gn), F32)],
        scratch_shapes=[pltpu.VMEM((rows, d), BF16), pltpu.VMEM((rows, 2 * gn // S5_KB), F32),
                        pltpu.VMEM((nb, gn), F32), pltpu.VMEM((nb, gn), F32)],
        compiler_params=_params("arbitrary"),
        name="s5_core",
    )(x, gpre, w["w_u"], s_re, s_im, w["bw"], w["cw"], w["ab_re"], w["ab_im"], w["d"])


def _s5_out_kernel(y_ref, x_ref, gpre_ref, wg_ref, gluw_ref, glub_ref, wout_ref, gpost_ref, xo_ref):
    x = x_ref[...]
    gate = _mm(_rms(x, gpre_ref[...]), wg_ref[...])
    y = y_ref[...]
    y = y * jax.nn.sigmoid(_mm(y, gluw_ref[...]) + glub_ref[...])
    y = y * _silu(gate)
    xo_ref[...] = x + _rms(_mm(y, wout_ref[...]), gpost_ref[...])


def _s5_out(y, x, gpre, gpost, w, *, rows):
    r, d = x.shape
    e = D_INNER
    row = lambda i: (i, 0)
    return pl.pallas_call(
        _s5_out_kernel,
        grid=(r // rows,),
        in_specs=[pl.BlockSpec((rows, e), row), pl.BlockSpec((rows, d), row), _const((1, d)),
                  _const((d, e)), _const((e, e)), _const((1, e)), _const((e, d)), _const((1, d))],
        out_specs=pl.BlockSpec((rows, d), row),
        out_shape=jax.ShapeDtypeStruct((r, d), F32),
        compiler_params=_params("parallel"),
        name="s5_out",
    )(y, x, gpre, w["w_gate"], w["glu_w"], w["glu_b"], w["w_out"], gpost)


def _rwkv_proj_kernel(x_ref, xprev_ref, gpre_ref, mu_ref, wr_ref, wk_ref, wv_ref, wg_ref,
                      w0_ref, w1_ref, w2_ref, a0_ref, a1_ref, a2_ref,
                      r_ref, k_ref, v_ref, g_ref, dec_ref, a_ref, shift_ref,
                      prev_ref, *, nb, tt):
    rows = nb * tt

    @pl.when(pl.program_id(0) == 0)
    def _():
        prev_ref[...] = xprev_ref[...]

    xn = _rms(x_ref[...], gpre_ref[...])
    if tt > 1:
        shifted = jnp.concatenate([prev_ref[...], xn[:rows - nb]], axis=0)
    else:
        shifted = prev_ref[...]
    prev_ref[...] = xn[rows - nb:]
    shift_ref[...] = xn[rows - nb:]
    xx = shifted - xn
    mix = lambda n: xn + xx * mu_ref[n:n + 1, :]
    r_ref[...] = _mm(mix(0), wr_ref[...])
    k_ref[...] = _mm(mix(2), wk_ref[...])
    v_ref[...] = _mm(mix(3), wv_ref[...])
    g_ref[...] = _silu(_mm(mix(5), wg_ref[...])).astype(g_ref.dtype)
    w_raw = w0_ref[...] + _mm(jnp.tanh(_mm(mix(1), w1_ref[...])), w2_ref[...])
    dec_ref[...] = jnp.exp(-(jax.nn.sigmoid(w_raw) * math.exp(-0.5)))
    a_ref[...] = jax.nn.sigmoid(a0_ref[...] + _mm(_mm(mix(4), a1_ref[...]), a2_ref[...]))


def _rwkv_proj(x, x_prev, gpre, w, *, nb, tt):
    r, d = x.shape
    e = D_INNER
    rows = nb * tt
    lora = w["w1"].shape[1]
    row = lambda i: (i, 0)
    big = pl.BlockSpec((rows, e), row)
    return pl.pallas_call(
        functools.partial(_rwkv_proj_kernel, nb=nb, tt=tt),
        grid=(r // rows,),
        in_specs=[pl.BlockSpec((rows, d), row), _const((nb, d)), _const((1, d)), _const((6, d)),
                  _const((d, e)), _const((d, e)), _const((d, e)), _const((d, e)),
                  _const((1, e)), _const((d, lora)), _const((lora, e)),
                  _const((1, e)), _const((d, lora)), _const((lora, e))],
        out_specs=[big, big, big, big, big, big, _const_out((nb, d))],
        out_shape=[jax.ShapeDtypeStruct((r, e), dt) for dt in (F32, F32, F32, BF16, F32, F32)]
                  + [jax.ShapeDtypeStruct((nb, d), F32)],
        scratch_shapes=[pltpu.VMEM((nb, d), F32)],
        compiler_params=_params("arbitrary"),
        name="rwkv_proj",
    )(x, x_prev, gpre, w["mu"], w["w_r"], w["w_k"], w["w_v"], w["w_g"],
      w["w0"], w["w1"], w["w2"], w["a0"], w["a1"], w["a2"])


RWKV_VSUB = 4
RWKV_VGRP = RWKV_HEAD // RWKV_VSUB
RWKV_BG = 8
RWKV_VBLK = 8
RWKV_UNROLL = 32


def _seg_sum(x):
    lane_axis = x.ndim - 1
    x = x + pltpu.roll(x, 2 * RWKV_HEADS, lane_axis)
    return x + pltpu.roll(x, RWKV_HEADS, lane_axis)


def _rwkv_rec_kernel(r_ref, w_ref, k_ref, v_ref, a_ref, vprev_ref, kk_ref, ka_ref, rk_ref,
                     kkrep_ref, karep_ref, lnw_ref, lnb_ref, s0_ref, o_ref, so_ref,
                     s_ref, vec_ref, sa_scale_ref, bon_ref, yraw_ref, *, tt):
    n = RWKV_HEAD
    nh = RWKV_HEADS
    shape = (RWKV_BG, LANES)
    step = pl.program_id(1)
    last = pl.num_programs(1) - 1

    @pl.when(step == 0)
    def _():
        s_ref[...] = s0_ref[0]
        yraw_ref[...] = jnp.zeros_like(yraw_ref)
        bon_ref[...] = jnp.zeros_like(bon_ref)

    tiles = [slice(j * LANES, (j + 1) * LANES) for j in range(RWKV_VGRP)]

    bon_prev = bon_ref[...]
    tot = yraw_ref[:, :, tiles[0]]
    for sl in tiles[1:]:
        tot = tot + yraw_ref[:, :, sl]
    mean = _seg_sum(tot) * (1.0 / n)
    sq = jnp.zeros((tt,) + shape, F32)
    for sl in tiles:
        yc = yraw_ref[:, :, sl] - mean
        sq = sq + yc * yc
    rstd = lax.rsqrt(_seg_sum(sq) * (1.0 / n) + RWKV_LN_EPS)
    for sl in tiles:
        o_ref[:, :, sl] = ((yraw_ref[:, :, sl] - mean) * rstd * lnw_ref[:, sl] + lnb_ref[:, sl]
                           + bon_prev * vprev_ref[:, :, sl])

    seg = lax.broadcasted_iota(jnp.int32, (tt,) + shape, 2) // nh
    seg_low = seg < 2
    seg_odd = (seg & 1) == 1

    def rep4(x):
        rolled = [x] + [pltpu.roll(x, q * nh, 2) for q in range(1, RWKV_VSUB)]
        pair = [jnp.where(seg_low, rolled[i], rolled[(i + 2) % RWKV_VSUB]) for i in range(RWKV_VSUB)]
        return [jnp.where(seg_odd, pair[(1 - s_) % RWKV_VSUB], pair[(-s_) % RWKV_VSUB])
                for s_ in range(RWKV_VSUB)]

    n2 = jnp.zeros((tt,) + shape, F32)
    bon = jnp.zeros((tt,) + shape, F32)
    for j, sl in enumerate(tiles):
        r = r_ref[:, :, sl]
        k = k_ref[:, :, sl]
        a = a_ref[:, :, sl]
        kk = k * kk_ref[:, sl]
        n2 = n2 + kk * kk
        bon = bon + r * (k * (1.0 + (a - 1.0) * ka_ref[:, sl])) * rk_ref[:, sl]
        r_rep = rep4(r)
        w_rep = rep4(w_ref[:, :, sl])
        k_rep = rep4(k)
        a_rep = rep4(a)
        for s_ in range(RWKV_VSUB):
            c = RWKV_VSUB * j + s_
            kk_c = k_rep[s_] * kkrep_ref[c]
            vec_ref[:, 0, c] = r_rep[s_]
            vec_ref[:, 1, c] = w_rep[s_]
            vec_ref[:, 2, c] = kk_c
            vec_ref[:, 3, c] = kk_c * a_rep[s_]
            vec_ref[:, 4, c] = k_rep[s_] * (1.0 + (a_rep[s_] - 1.0) * karep_ref[c])
    inv = lax.rsqrt(jnp.maximum(_seg_sum(n2), 1e-24))
    sa_scale_ref[...] = -(inv * inv)
    bon_ref[...] = _seg_sum(bon)

    def token(t, carry):
        sa_scale = sa_scale_ref[t]
        zero = jnp.zeros((RWKV_VBLK,) + shape, F32)
        for vb in range(RWKV_VGRP // RWKV_VBLK):
            vsl = slice(vb * RWKV_VBLK, (vb + 1) * RWKV_VBLK)
            vv = jnp.stack([v_ref[t, :, sl] for sl in tiles[vsl]])

            def sa_step(c, acc):
                return acc + s_ref[c, vsl] * vec_ref[t, 2, c]

            sa = lax.fori_loop(0, n, sa_step, zero, unroll=RWKV_UNROLL) * sa_scale

            def update_step(c, acc):
                s_new = s_ref[c, vsl] * vec_ref[t, 1, c] + sa * vec_ref[t, 3, c] + vv * vec_ref[t, 4, c]
                s_ref[c, vsl] = s_new
                return acc + s_new * vec_ref[t, 0, c]

            y = lax.fori_loop(0, n, update_step, zero, unroll=RWKV_UNROLL)
            for i, sl in enumerate(tiles[vsl]):
                yraw_ref[t, :, sl] = y[i]
        return carry

    @pl.when(step < last)
    def _():
        lax.fori_loop(0, tt, token, 0)

    @pl.when(step == last)
    def _():
        so_ref[0] = s_ref[...]


def _rwkv_rec(r, w, k, v, a, pk, s0, *, tt):
    t_len, nb, e = r.shape
    n = RWKV_HEAD
    n_tiles = t_len // tt
    cur = pl.BlockSpec((tt, RWKV_BG, e), lambda bi, si: (jnp.minimum(si, n_tiles - 1), bi, 0))
    prev = pl.BlockSpec((tt, RWKV_BG, e), lambda bi, si: (jnp.maximum(si - 1, 0), bi, 0))
    st = pl.BlockSpec((1, n, RWKV_VGRP, RWKV_BG, LANES), lambda bi, si: (bi, 0, 0, 0, 0))
    return pl.pallas_call(
        functools.partial(_rwkv_rec_kernel, tt=tt),
        grid=(nb // RWKV_BG, n_tiles + 1),
        in_specs=[cur] * 5 + [prev] + [_const((RWKV_BG, e))] * 3
                 + [_const((n, RWKV_BG, LANES))] * 2 + [_const((RWKV_BG, e))] * 2 + [st],
        out_specs=[prev, st],
        out_shape=[jax.ShapeDtypeStruct((t_len, nb, e), F32), jax.ShapeDtypeStruct(s0.shape, F32)],
        scratch_shapes=[pltpu.VMEM((n, RWKV_VGRP, RWKV_BG, LANES), F32),
                        pltpu.VMEM((tt, 5, n, RWKV_BG, LANES), F32),
                        pltpu.VMEM((tt, RWKV_BG, LANES), F32),
                        pltpu.VMEM((tt, RWKV_BG, LANES), F32),
                        pltpu.VMEM((tt, RWKV_BG, e), F32)],
        compiler_params=_params("parallel", "arbitrary"),
        name="rwkv_recurrence",
    )(r, w, k, v, a, v, pk["k_k"], pk["k_a"], pk["r_k"], pk["k_k_rep"], pk["k_a_rep"],
      pk["ln_w"], pk["ln_b"], s0)


def _gated_out_kernel(y_ref, g_ref, x_ref, w_ref, gpost_ref, xo_ref):
    y = y_ref[...] * g_ref[...].astype(F32)
    xo_ref[...] = x_ref[...] + _rms(_mm(y, w_ref[...]), gpost_ref[...])


def _gated_out(y, g, x, w, gpost, *, rows):
    r, d = x.shape
    e = D_INNER
    row = lambda i: (i, 0)
    return pl.pallas_call(
        _gated_out_kernel,
        grid=(r // rows,),
        in_specs=[pl.BlockSpec((rows, e), row), pl.BlockSpec((rows, e), row),
                  pl.BlockSpec((rows, d), row), _const((e, d)), _const((1, d))],
        out_specs=pl.BlockSpec((rows, d), row),
        out_shape=jax.ShapeDtypeStruct((r, d), F32),
        compiler_params=_params("parallel"),
        name="gated_out",
    )(y, g, x, w, gpost)


def _ret_proj_kernel(x_ref, gpre_ref, wq_ref, wk_ref, wv_ref, wg_ref, q_ref, k_ref, v_ref, g_ref):
    xn = _rms(x_ref[...], gpre_ref[...]).astype(BF16)
    dt = q_ref.dtype
    q_ref[...] = jnp.dot(xn, wq_ref[...], preferred_element_type=F32).astype(dt)
    k_ref[...] = jnp.dot(xn, wk_ref[...], preferred_element_type=F32).astype(dt)
    v_ref[...] = jnp.dot(xn, wv_ref[...], preferred_element_type=F32).astype(dt)
    g_ref[...] = _silu(jnp.dot(xn, wg_ref[...], preferred_element_type=F32)).astype(dt)


def _ret_proj(x, gpre, w, *, rows, out_dtype):
    r, d = x.shape
    e = D_INNER
    row = lambda i: (i, 0)
    return pl.pallas_call(
        _ret_proj_kernel,
        grid=(r // rows,),
        in_specs=[pl.BlockSpec((rows, d), row), _const((1, d)), _const((d, d)), _const((d, d)),
                  _const((d, e)), _const((d, e))],
        out_specs=[pl.BlockSpec((rows, d), row), pl.BlockSpec((rows, d), row),
                   pl.BlockSpec((rows, e), row), pl.BlockSpec((rows, e), row)],
        out_shape=[jax.ShapeDtypeStruct((r, d), out_dtype), jax.ShapeDtypeStruct((r, d), out_dtype),
                   jax.ShapeDtypeStruct((r, e), out_dtype), jax.ShapeDtypeStruct((r, e), out_dtype)],
        compiler_params=_params("parallel"),
        name="ret_proj",
    )(x, gpre, w["w_q"], w["w_k"], w["w_v"], w["w_g"])


def _ret_core_kernel(q_ref, kt_ref, v_ref, cos_ref, sin_ref, cost_ref, sint_ref, mask_ref,
                     qdec_ref, kdec_ref, gl_ref, s0_ref, *rest, fuse_out):
    half = RET_DK // 2
    if fuse_out:
        g_ref, x_ref, wo_ref, gpost_ref, out_ref, so_ref, s_ref = rest
    else:
        out_ref, so_ref, s_ref = rest

    @pl.when(pl.program_id(1) == 0)
    def _():
        s_ref[...] = s0_ref[0]

    cos = cos_ref[...]
    sin = sin_ref[...]
    cost = cost_ref[...]
    sint = sint_ref[...]
    proj = None
    for h in range(RET_HEADS):
        q1 = q_ref[:, h * RET_DK:h * RET_DK + half].astype(F32)
        q2 = q_ref[:, h * RET_DK + half:(h + 1) * RET_DK].astype(F32)
        qr = jnp.concatenate([q1 * cos - q2 * sin, q2 * cos + q1 * sin], axis=1)
        k1 = kt_ref[0, h * RET_DK:h * RET_DK + half, :].astype(F32)
        k2 = kt_ref[0, h * RET_DK + half:(h + 1) * RET_DK, :].astype(F32)
        krt = jnp.concatenate([k1 * cost - k2 * sint, k2 * cost + k1 * sint], axis=0)
        krt = krt * (RET_DK ** -0.5)
        vh = v_ref[:, h * RET_DV:(h + 1) * RET_DV].astype(BF16)
        qb = qr.astype(BF16)
        scores = jnp.dot(qb, krt.astype(BF16), preferred_element_type=F32) * mask_ref[h]
        s_old = s_ref[h]
        y = jnp.dot(scores.astype(BF16), vh, preferred_element_type=F32)
        y = y + jnp.dot(qb, s_old.astype(BF16), preferred_element_type=F32) * qdec_ref[h]
        kw = (krt * kdec_ref[h]).astype(BF16)
        s_ref[h] = gl_ref[h] * s_old + jnp.dot(kw, vh, preferred_element_type=F32)
        y = y * lax.rsqrt(jnp.mean(y * y, axis=-1, keepdims=True) + NORM_EPS)
        if fuse_out:
            yg = y * g_ref[:, h * RET_DV:(h + 1) * RET_DV].astype(F32)
            part = _mm(yg, wo_ref[h * RET_DV:(h + 1) * RET_DV, :])
            proj = part if proj is None else proj + part
        else:
            out_ref[:, h * RET_DV:(h + 1) * RET_DV] = y
    if fuse_out:
        out_ref[...] = x_ref[...] + _rms(proj, gpost_ref[...])

    @pl.when(pl.program_id(1) == pl.num_programs(1) - 1)
    def _():
        so_ref[0] = s_ref[...]


def _ret_core(q, kt, v, tabs, s0, *, lc, out=None, drop_first=False):
    b, dkh, t_len = kt.shape
    nc = t_len // lc
    e = D_INNER
    d = D_MODEL
    rowc = lambda bi, ci: (bi * nc + ci, 0)
    state = pl.BlockSpec((1, RET_HEADS, RET_DK, RET_DV), lambda bi, ci: (bi, 0, 0, 0))
    half = RET_DK // 2
    in_specs = [pl.BlockSpec((lc, dkh), rowc),
                pl.BlockSpec((1, dkh, lc), lambda bi, ci: (bi, 0, ci)),
                pl.BlockSpec((lc, e), rowc),
                pl.BlockSpec((lc, half), lambda bi, ci: (ci, 0)),
                pl.BlockSpec((lc, half), lambda bi, ci: (ci, 0)),
                pl.BlockSpec((half, lc), lambda bi, ci: (0, ci)),
                pl.BlockSpec((half, lc), lambda bi, ci: (0, ci)),
                _const((RET_HEADS, lc, lc)), _const((RET_HEADS, lc, 1)),
                _const((RET_HEADS, 1, lc)), _const((RET_HEADS, 1, 1)), state]
    args = [q, kt, v, tabs["cos"], tabs["sin"], tabs["cos_t"], tabs["sin_t"], tabs["mask"],
            tabs["qdec"], tabs["kdec"], tabs["gl"], s0]
    width = e
    if out is not None:
        in_specs += [pl.BlockSpec((lc, e), rowc), pl.BlockSpec((lc, d), rowc), _const((e, d)),
                     _const((1, d))]
        args += list(out)
        width = d
    out_rows, out_map = b * t_len, rowc
    if drop_first:
        out_rows = b * (t_len - lc)
        out_map = lambda bi, ci: (bi * (nc - 1) + jnp.maximum(ci - 1, 0), 0)
    return pl.pallas_call(
        functools.partial(_ret_core_kernel, fuse_out=out is not None),
        grid=(b, nc),
        in_specs=in_specs,
        out_specs=[pl.BlockSpec((lc, width), out_map), state],
        out_shape=[jax.ShapeDtypeStruct((out_rows, width), F32),
                   jax.ShapeDtypeStruct((b, RET_HEADS, RET_DK, RET_DV), F32)],
        scratch_shapes=[pltpu.VMEM((RET_HEADS, RET_DK, RET_DV), F32)],
        compiler_params=_params("parallel", "arbitrary"),
        name="ret_core",
    )(*args)


def _ret_tables(pos, lc):
    half = RET_DK // 2
    inv = ROPE_BASE ** (-jnp.arange(half, dtype=F32) / half)
    ang = pos.astype(F32)[:, None] * inv[None, :]
    cos = jnp.cos(ang)
    sin = jnp.sin(ang)
    log_g = jnp.log1p(-jnp.exp2(-5.0 - jnp.arange(RET_HEADS, dtype=F32)))
    n = jnp.arange(lc, dtype=F32)
    diff = n[:, None] - n[None, :]
    mask = jnp.where(diff[None] >= 0, jnp.exp(diff[None] * log_g[:, None, None]), 0.0)
    qdec = jnp.exp((n[None, :] + 1.0) * log_g[:, None])[:, :, None]
    kdec = jnp.exp((lc - 1.0 - n)[None, :] * log_g[:, None])[:, None, :]
    gl = jnp.exp(lc * log_g)[:, None, None]
    return dict(cos=cos, sin=sin, cos_t=cos.T, sin_t=sin.T, mask=mask, qdec=qdec, kdec=kdec, gl=gl)


def _s5_block_weights(bb_re, bb_im, c_re, c_im):
    gl = S5_GROUPS // S5_KB
    eye = jnp.eye(gl, dtype=BF16)

    def bdiag_b(bb):
        x = bb.astype(BF16).reshape(S5_KB, gl, S5_GROUP, S5_STATE)
        x = x[:, :, :, None, :] * eye[None, :, None, :, None]
        return x.reshape(S5_KB, gl * S5_GROUP, gl * S5_STATE)

    def bdiag_c(c):
        x = jnp.swapaxes(c, 1, 2).astype(BF16).reshape(S5_KB, gl, S5_STATE, S5_GROUP)
        x = x[:, :, :, None, :] * eye[None, :, None, :, None]
        return x.reshape(S5_KB, gl * S5_STATE, gl * S5_GROUP)

    bw = jnp.concatenate([bdiag_b(bb_re), bdiag_b(bb_im)], axis=2)
    cw = jnp.concatenate([bdiag_c(c_re), -bdiag_c(c_im)], axis=1)
    return bw, cw


def _head_minor(x):
    lead = x.shape[:-1]
    x = x.reshape(lead + (RWKV_HEADS, RWKV_HEAD))
    return jnp.swapaxes(x, -1, -2).reshape(lead + (D_INNER,))


def _key_param(p):
    return jnp.broadcast_to(p.reshape(1, D_INNER), (RWKV_BG, D_INNER))


def _key_param_rep(p):
    x = jnp.tile(p.reshape(RWKV_HEAD, 1, RWKV_HEADS), (1, 1, RWKV_VSUB))
    return jnp.broadcast_to(x, (RWKV_HEAD, RWKV_BG, LANES))


def _wkv_to_blocks(s, nb):
    x = s.reshape(nb // RWKV_BG, RWKV_BG, RWKV_HEADS, RWKV_VGRP, RWKV_VSUB, RWKV_HEAD)
    x = jnp.transpose(x, (0, 5, 3, 1, 4, 2))
    return x.reshape(nb // RWKV_BG, RWKV_HEAD, RWKV_VGRP, RWKV_BG, LANES)


def _wkv_from_blocks(x, nb):
    x = x.reshape(nb // RWKV_BG, RWKV_HEAD, RWKV_VGRP, RWKV_BG, RWKV_VSUB, RWKV_HEADS)
    x = jnp.transpose(x, (0, 3, 5, 2, 4, 1))
    return x.reshape(nb, RWKV_HEADS, RWKV_HEAD, RWKV_HEAD)


def _run_stream(x, states, wts, norm_pre, norm_post, *, nb, t_len, pos0, tiles, n_drop):
    d = D_MODEL
    e = D_INNER
    new = {}
    gp = lambda i: (norm_pre[i].reshape(1, d), norm_post[i].reshape(1, d))

    gpre, gpost = gp(0)
    cs = jnp.transpose(states["lru_conv"], (1, 0, 2)).reshape((LRU_CONV_W - 1) * nb, e)
    x, cso, ho = _lru_layer(x, cs, states["lru_h"], gpre, gpost, wts["lru"], nb=nb, tt=tiles["lru_tt"])
    new["lru_conv"] = jnp.transpose(cso.reshape(LRU_CONV_W - 1, nb, e), (1, 0, 2))
    new["lru_h"] = ho

    gpre, gpost = gp(1)
    w = wts["s5"]
    gn = S5_GROUPS * S5_STATE
    y, sre, sim = _s5_core(x, gpre, w, states["s5_re"].reshape(nb, gn),
                           states["s5_im"].reshape(nb, gn), nb=nb, tt=tiles["s5_tt"])
    x = _s5_out(y, x, gpre, gpost, w, rows=tiles["rows"])
    new["s5_re"] = sre.reshape(nb, S5_GROUPS, S5_STATE)
    new["s5_im"] = sim.reshape(nb, S5_GROUPS, S5_STATE)

    gpre, gpost = gp(2)
    w = wts["rwkv"]
    r, k, v, g, dec, a, shift = _rwkv_proj(x, states["rwkv_shift"], gpre, w, nb=nb, tt=tiles["rwkv_tt"])
    as3 = lambda z: z.reshape(t_len, nb, e)
    pk = dict(k_k=_key_param(w["k_k"]), k_a=_key_param(w["k_a"]), r_k=_key_param(w["r_k"]),
              k_k_rep=_key_param_rep(w["k_k"]), k_a_rep=_key_param_rep(w["k_a"]),
              ln_w=_key_param(w["ln_w"]), ln_b=_key_param(w["ln_b"]))
    o, s_new = _rwkv_rec(as3(r), as3(dec), as3(k), as3(v), as3(a), pk,
                         _wkv_to_blocks(states["rwkv_wkv"], nb), tt=tiles["rec_tt"])
    x = _gated_out(o.reshape(t_len * nb, e), g, x, w["w_o"], gpost, rows=tiles["rows"])
    new["rwkv_shift"] = shift
    new["rwkv_wkv"] = _wkv_from_blocks(s_new, nb)

    gpre, gpost = gp(3)
    w = wts["ret"]
    lc = tiles["ret_lc"]
    t_pad = -(-t_len // lc) * lc
    pad = t_pad - t_len
    xb = jnp.transpose(x.reshape(t_len, nb, d), (1, 0, 2))
    xb = jnp.pad(xb, ((0, 0), (pad, 0), (0, 0))).reshape(nb * t_pad, d)
    proj_dtype = BF16 if lc % 16 == 0 else F32
    q, kk, v, g = _ret_proj(xb, gpre, w, rows=tiles["ret_rows"], out_dtype=proj_dtype)
    kt = jnp.transpose(kk.reshape(nb, t_pad, RET_HEADS * RET_DK), (0, 2, 1))
    tabs = _ret_tables(pos0 - pad + jnp.arange(t_pad, dtype=jnp.int32), lc)
    if lc % LANES == 0:
        drop_first = pad + n_drop == lc
        xb, s_new = _ret_core(q, kt, v, tabs, states["ret"], lc=lc, out=(g, xb, w["w_o"], gpost),
                              drop_first=drop_first)
        xb = xb.reshape(nb, -1, d)
        if not drop_first:
            xb = xb[:, pad + n_drop:]
    else:
        y, s_new = _ret_core(q, kt, v, tabs, states["ret"], lc=lc)
        xb = _gated_out(y, g, xb, w["w_o"], gpost, rows=tiles["ret_rows"])
        xb = xb.reshape(nb, t_pad, d)[:, pad + n_drop:]
    new["ret"] = s_new
    return xb, new


def kernel(x_prompt, x_sample, state_lru_conv, state_lru_h, state_s5_re, state_s5_im, state_rwkv_shift, state_rwkv_wkv, state_ret, meta_tokens, norm_pre, norm_post, lru_w_in, lru_conv_w, lru_conv_b, lru_wa, lru_ba, lru_wx, lru_bx, lru_lam, lru_w_out, s5_w_in, s5_log_dt, s5_a_re, s5_a_im, s5_b_re, s5_b_im, s5_c_re, s5_c_im, s5_d, s5_glu_w, s5_glu_b, s5_w_out, rwkv_mu, rwkv_w_r, rwkv_w_k, rwkv_w_v, rwkv_w_g, rwkv_w0, rwkv_w1, rwkv_w2, rwkv_a0, rwkv_a1, rwkv_a2, rwkv_k_k, rwkv_k_a, rwkv_r_k, rwkv_ln_w, rwkv_ln_b, rwkv_w_o, ret_w_q, ret_w_k, ret_w_v, ret_w_g, ret_w_o):
    d = D_MODEL
    e = D_INNER
    bp, seq, _ = x_prompt.shape
    bs, dec_seq, _ = x_sample.shape
    t_p = seq + N_META
    bf = lambda x: x.astype(BF16)
    vec = lambda x: x.reshape(1, -1)
    hm = _head_minor

    ab_re, ab_im, bb_re, bb_im = _s5_discretize(
        s5_log_dt[0], s5_a_re[0], s5_a_im[0],
        jnp.swapaxes(s5_b_re[0], 1, 2), jnp.swapaxes(s5_b_im[0], 1, 2))
    bw, cw = _s5_block_weights(bb_re, bb_im, s5_c_re[0], s5_c_im[0])
    wts = dict(
        lru=dict(w_in=bf(lru_w_in[0]), conv_w=lru_conv_w[0], conv_b=vec(lru_conv_b[0]),
                 wa=bf(lru_wa[0]), ba=vec(lru_ba[0]), wx=bf(lru_wx[0]), bx=vec(lru_bx[0]),
                 lam=vec(lru_lam[0]), w_out=bf(lru_w_out[0])),
        s5=dict(w_u=bf(s5_w_in[0, :, :e]), w_gate=bf(s5_w_in[0, :, e:]), bw=bw, cw=cw, ab_re=vec(ab_re), ab_im=vec(ab_im),
                d=vec(s5_d[0]), glu_w=bf(s5_glu_w[0]), glu_b=vec(s5_glu_b[0]), w_out=bf(s5_w_out[0])),
        rwkv=dict(mu=rwkv_mu[0], w_r=bf(hm(rwkv_w_r[0])), w_k=bf(hm(rwkv_w_k[0])),
                  w_v=bf(hm(rwkv_w_v[0])), w_g=bf(hm(rwkv_w_g[0])), w0=vec(hm(rwkv_w0[0])),
                  w1=bf(rwkv_w1[0]), w2=bf(hm(rwkv_w2[0])), a0=vec(hm(rwkv_a0[0])),
                  a1=bf(rwkv_a1[0]), a2=bf(hm(rwkv_a2[0])),
                  k_k=hm(rwkv_k_k[0]), k_a=hm(rwkv_k_a[0]), r_k=hm(rwkv_r_k[0]),
                  ln_w=hm(rwkv_ln_w[0]), ln_b=hm(rwkv_ln_b[0]), w_o=bf(hm(rwkv_w_o[0].T).T)),
        ret=dict(w_q=bf(ret_w_q[0]), w_k=bf(ret_w_k[0]), w_v=bf(ret_w_v[0]), w_g=bf(ret_w_g[0]),
                 w_o=bf(ret_w_o[0])),
    )

    meta = jnp.broadcast_to(meta_tokens[None].astype(x_prompt.dtype), (bp, N_META, d))
    hp = jnp.concatenate([meta, x_prompt], axis=1)
    z = lambda *s: jnp.zeros(s, F32)
    p_states = dict(lru_conv=z(bp, LRU_CONV_W - 1, e), lru_h=z(bp, e),
                    s5_re=z(bp, S5_GROUPS, S5_STATE), s5_im=z(bp, S5_GROUPS, S5_STATE),
                    rwkv_shift=z(bp, d), rwkv_wkv=z(bp, RWKV_HEADS, RWKV_HEAD, RWKV_HEAD),
                    ret=z(bp, RET_HEADS, RET_DK, RET_DV))
    p_tiles = dict(lru_tt=_divisor_tile(t_p, 48, 1), s5_tt=_divisor_tile(t_p, 48, 1),
                   rwkv_tt=_divisor_tile(t_p, 24, 1), rec_tt=_divisor_tile(t_p, 16, 1),
                   rows=_divisor_tile(t_p * bp, 384, 8), ret_lc=RET_CHUNK,
                   ret_rows=_divisor_tile(bp * (-(-t_p // RET_CHUNK) * RET_CHUNK), 512, 8))
    yp, new_p = _run_stream(hp, p_states, wts, norm_pre, norm_post, nb=bp, t_len=t_p, pos0=0,
                            tiles=p_tiles, n_drop=N_META)

    s_states = dict(lru_conv=state_lru_conv[0], lru_h=state_lru_h[0], s5_re=state_s5_re[0],
                    s5_im=state_s5_im[0], rwkv_shift=state_rwkv_shift[0],
                    rwkv_wkv=state_rwkv_wkv[0], ret=state_ret[0])
    s_rows = _divisor_tile(dec_seq * bs, 256, 8)
    s_tiles = dict(lru_tt=2, s5_tt=1, rwkv_tt=2, rec_tt=dec_seq, rows=s_rows, ret_lc=dec_seq,
                   ret_rows=s_rows)
    ys, new_s = _run_stream(x_sample, s_states, wts, norm_pre, norm_post, nb=bs, t_len=dec_seq,
                            pos0=PAST_LEN, tiles=s_tiles, n_drop=0)

    names = ("lru_conv", "lru_h", "s5_re", "s5_im", "rwkv_shift", "rwkv_wkv", "ret")
    return ((yp, ys)
            + tuple(new_p[n][None] for n in names)
            + tuple(new_s[n][None] for n in names))
```

```python
import functools
import math

import jax
import jax.numpy as jnp
from jax import lax
from jax.experimental import pallas as pl
from jax.experimental.pallas import tpu as pltpu

F32 = jnp.float32
BF16 = jnp.bfloat16

D_MODEL = 1024
D_INNER = 2048
N_META = 16
NORM_EPS = 1e-6
LRU_CONV_W = 4
LRU_BLOCKS = 16
LRU_BLOCK = 128
LRU_C = 8.0
S5_GROUP = 16
S5_GROUPS = 128
S5_STATE = 64
S5_KB = 8
RWKV_HEAD = 64
RWKV_HEADS = 32
RWKV_LN_EPS = 64e-5
RET_HEADS = 4
RET_DK = 256
RET_DV = 512
ROPE_BASE = 10000.0
PAST_LEN = 16384
RET_CHUNK = 256

LANES = 128
VMEM_LIMIT = 56 * 1024 * 1024


def _params(*sem):
    return pltpu.CompilerParams(dimension_semantics=sem, vmem_limit_bytes=VMEM_LIMIT)


def _const(shape):
    zeros = (0,) * len(shape)
    return pl.BlockSpec(shape, lambda *_: zeros, pipeline_mode=pl.Buffered(1))


def _const_out(shape):
    zeros = (0,) * len(shape)
    return pl.BlockSpec(shape, lambda *_: zeros)


def _divisor_tile(n, pref, mult):
    best = mult
    t = mult
    while t <= min(n, pref):
        if n % t == 0:
            best = t
        t += mult
    assert n % best == 0
    return best


def _rms(x, g):
    return x * lax.rsqrt(jnp.mean(x * x, axis=-1, keepdims=True) + NORM_EPS) * g


def _mm(a, w):
    return jnp.dot(a.astype(BF16), w, preferred_element_type=F32)


def _silu(x):
    return x * jax.nn.sigmoid(x)


def _softplus(x):
    return jnp.maximum(x, 0.0) + jnp.log1p(jnp.exp(-jnp.abs(x)))


def _lru_kernel(x_ref, cs_ref, h0_ref, gpre_ref, gpost_ref, win_ref, cw_ref, cb_ref,
                wa_ref, ba_ref, wx_ref, bx_ref, lam_ref, wout_ref,
                xo_ref, cso_ref, ho_ref,
                uz_ref, b_ref, tail_ref, h_ref, *, nb, tt):
    rows = nb * tt
    e = D_INNER

    @pl.when(pl.program_id(0) == 0)
    def _():
        tail_ref[...] = cs_ref[...]
        h_ref[...] = h0_ref[...]

    d = D_MODEL
    x = jnp.concatenate([x_ref[:, t * d:(t + 1) * d] for t in range(tt)], axis=0)
    xn = _rms(x, gpre_ref[...])
    uz_ref[...] = _mm(xn, win_ref[...])

    for j in range(LRU_BLOCKS):
        sl = slice(j * LRU_BLOCK, (j + 1) * LRU_BLOCK)
        ext = jnp.concatenate([tail_ref[:, sl], uz_ref[:, sl]], axis=0)
        cw = cw_ref[:, sl]
        xc = cb_ref[:, sl]
        for jj in range(LRU_CONV_W):
            xc = xc + ext[jj * nb:jj * nb + rows] * cw[jj:jj + 1]
        tail_ref[:, sl] = ext[rows:rows + (LRU_CONV_W - 1) * nb]
        gate_r = jax.nn.sigmoid(_mm(xc, wa_ref[j]) + ba_ref[:, sl])
        gate_i = jax.nn.sigmoid(_mm(xc, wx_ref[j]) + bx_ref[:, sl])
        log_a = -LRU_C * gate_r * _softplus(-lam_ref[:, sl])
        a = jnp.exp(log_a)
        bx = jnp.sqrt(-jnp.tanh(log_a) * (a * a + 1.0)) * gate_i * xc
        h = h_ref[:, sl]
        hs = []
        for t in range(tt):
            h = a[t * nb:(t + 1) * nb] * h + bx[t * nb:(t + 1) * nb]
            hs.append(h)
        h_ref[:, sl] = h
        b_ref[:, sl] = jnp.concatenate(hs, axis=0)

    y = b_ref[...] * _silu(uz_ref[:, e:])
    out = _mm(y, wout_ref[...])
    xo_ref[...] = x + _rms(out, gpost_ref[...])
    cso_ref[...] = tail_ref[...]
    ho_ref[...] = h_ref[...]


def _lru_layer(x, conv_state, h0, gpre, gpost, w, *, nb, tt):
    _, t_len, d = x.shape
    r = nb * t_len
    e = D_INNER
    rows = nb * tt
    ctail = (LRU_CONV_W - 1) * nb
    row = lambda i: (i, 0)
    return pl.pallas_call(
        functools.partial(_lru_kernel, nb=nb, tt=tt),
        grid=(r // rows,),
        in_specs=[
            pl.BlockSpec((nb, tt * d), lambda i: (0, i)), _const((ctail, e)), _const((nb, e)),
            _const((1, d)), _const((1, d)), _const((d, 2 * e)),
            _const((LRU_CONV_W, e)), _const((1, e)),
            _const((LRU_BLOCKS, LRU_BLOCK, LRU_BLOCK)), _const((1, e)),
            _const((LRU_BLOCKS, LRU_BLOCK, LRU_BLOCK)), _const((1, e)),
            _const((1, e)), _const((e, d)),
        ],
        out_specs=[pl.BlockSpec((rows, d), row), _const_out((ctail, e)), _const_out((nb, e))],
        out_shape=[jax.ShapeDtypeStruct((r, d), F32),
                   jax.ShapeDtypeStruct((ctail, e), F32),
                   jax.ShapeDtypeStruct((nb, e), F32)],
        scratch_shapes=[pltpu.VMEM((rows, 2 * e), F32), pltpu.VMEM((rows, e), F32),
                        pltpu.VMEM((ctail, e), F32),
                        pltpu.VMEM((nb, e), F32)],
        compiler_params=_params("arbitrary"),
        name="lru_layer",
    )(x.reshape(nb, t_len * d), conv_state, h0, gpre, gpost, w["w_in"], w["conv_w"], w["conv_b"],
      w["wa"], w["ba"], w["wx"], w["bx"], w["lam"], w["w_out"])


def _s5_disc_kernel(logdt_ref, are_ref, aim_ref, bre_ref, bim_ref,
                    abre_ref, abim_ref, bbre_ref, bbim_ref):
    dt = jnp.exp(logdt_ref[...])
    a_re = are_ref[...]
    a_im = aim_ref[...]
    mag = jnp.exp(dt * a_re)
    ang = dt * a_im
    ab_re = mag * jnp.cos(ang)
    ab_im = mag * jnp.sin(ang)
    den = a_re * a_re + a_im * a_im
    f_re = ((ab_re - 1.0) * a_re + ab_im * a_im) / den
    f_im = (ab_im * a_re - (ab_re - 1.0) * a_im) / den
    abre_ref[...] = ab_re
    abim_ref[...] = ab_im
    b_re = bre_ref[...]
    b_im = bim_ref[...]
    fr = f_re[:, None, :]
    fi = f_im[:, None, :]
    bbre_ref[...] = fr * b_re - fi * b_im
    bbim_ref[...] = fr * b_im + fi * b_re


def _s5_discretize(log_dt, a_re, a_im, b_re, b_im):
    g, n = a_re.shape
    c = b_re.shape[1]
    return pl.pallas_call(
        _s5_disc_kernel,
        out_shape=[jax.ShapeDtypeStruct((g, n), F32), jax.ShapeDtypeStruct((g, n), F32),
                   jax.ShapeDtypeStruct((g, c, n), F32), jax.ShapeDtypeStruct((g, c, n), F32)],
        name="s5_discretize",
    )(log_dt.reshape(g, 1), a_re, a_im, b_re, b_im)


def _gelu_tanh(x):
    return 0.5 * x * (1.0 + jnp.tanh(math.sqrt(2.0 / math.pi) * (x + 0.044715 * (x * x * x))))


def _s5_core_kernel(x_ref, gpre_ref, wu_ref, sre_ref, sim_ref, bw_ref, cw_ref, abre_ref, abim_ref,
                    d_ref, y_ref, sreo_ref, simo_ref,
                    xn_ref, bu_ref, xr_ref, xi_ref, *, nb, tt):
    gn = S5_GROUPS * S5_STATE // S5_KB
    ch = D_INNER // S5_KB

    @pl.when(pl.program_id(0) == 0)
    def _():
        xr_ref[...] = sre_ref[...]
        xi_ref[...] = sim_ref[...]

    xn_ref[...] = _rms(x_ref[...], gpre_ref[...]).astype(BF16)

    for kb in range(S5_KB):
        lanes = slice(kb * gn, (kb + 1) * gn)
        cols = slice(kb * ch, (kb + 1) * ch)
        u = jnp.dot(xn_ref[...], wu_ref[:, cols], preferred_element_type=F32)
        bu_ref[...] = _mm(u, bw_ref[kb])
        ar = abre_ref[:, lanes]
        ai = abim_ref[:, lanes]

        def step(t, carry):
            xr, xi = carry
            r = pl.ds(pl.multiple_of(t * nb, nb), nb)
            nr = ar * xr - ai * xi + bu_ref[r, :gn]
            ni = ar * xi + ai * xr + bu_ref[r, gn:]
            bu_ref[r, :gn] = nr
            bu_ref[r, gn:] = ni
            return nr, ni

        xr, xi = lax.fori_loop(0, tt, step, (xr_ref[:, lanes], xi_ref[:, lanes]), unroll=True)
        xr_ref[:, lanes] = xr
        xi_ref[:, lanes] = xi
        y = _mm(bu_ref[...], cw_ref[kb]) + d_ref[:, cols] * u
        y_ref[:, cols] = _gelu_tanh(y)

    sreo_ref[...] = xr_ref[...]
    simo_ref[...] = xi_ref[...]


def _s5_core(x, gpre, w, s_re, s_im, *, nb, tt):
    r, d = x.shape
    e = D_INNER
    gn = S5_GROUPS * S5_STATE
    rows = nb * tt
    row = lambda i: (i, 0)
    return pl.pallas_call(
        functools.partial(_s5_core_kernel, nb=nb, tt=tt),
        grid=(r // rows,),
        in_specs=[pl.BlockSpec((rows, d), row), _const((1, d)), _const((d, e)),
                  _const((nb, gn)), _const((nb, gn)), _const(w["bw"].shape), _const(w["cw"].shape),
                  _const((1, gn)), _const((1, gn)), _const((1, e))],
        out_specs=[pl.BlockSpec((rows, e), row), _const_out((nb, gn)), _const_out((nb, gn))],
        out_shape=[jax.ShapeDtypeStruct((r, e), F32), jax.ShapeDtypeStruct((nb, gn), F32),
                   jax.ShapeDtypeStruct((nb, gn), F32)],
        scratch_shapes=[pltpu.VMEM((rows, d), BF16), pltpu.VMEM((rows, 2 * gn // S5_KB), F32),
                        pltpu.VMEM((nb, gn), F32), pltpu.VMEM((nb, gn), F32)],
        compiler_params=_params("arbitrary"),
        name="s5_core",
    )(x, gpre, w["w_u"], s_re, s_im, w["bw"], w["cw"], w["ab_re"], w["ab_im"], w["d"])


def _s5_out_kernel(y_ref, x_ref, gpre_ref, wg_ref, gluw_ref, glub_ref, wout_ref, gpost_ref, xo_ref):
    x = x_ref[...]
    gate = _mm(_rms(x, gpre_ref[...]), wg_ref[...])
    y = y_ref[...]
    y = y * jax.nn.sigmoid(_mm(y, gluw_ref[...]) + glub_ref[...])
    y = y * _silu(gate)
    xo_ref[...] = x + _rms(_mm(y, wout_ref[...]), gpost_ref[...])


def _s5_out(y, x, gpre, gpost, w, *, rows):
    r, d = x.shape
    e = D_INNER
    row = lambda i: (i, 0)
    return pl.pallas_call(
        _s5_out_kernel,
        grid=(r // rows,),
        in_specs=[pl.BlockSpec((rows, e), row), pl.BlockSpec((rows, d), row), _const((1, d)),
                  _const((d, e)), _const((e, e)), _const((1, e)), _const((e, d)), _const((1, d))],
        out_specs=pl.BlockSpec((rows, d), row),
        out_shape=jax.ShapeDtypeStruct((r, d), F32),
        compiler_params=_params("parallel"),
        name="s5_out",
    )(y, x, gpre, w["w_gate"], w["glu_w"], w["glu_b"], w["w_out"], gpost)


def _rwkv_proj_kernel(x_ref, xprev_ref, gpre_ref, mu_ref, wr_ref, wk_ref, wv_ref, wg_ref,
                      w0_ref, w1_ref, w2_ref, a0_ref, a1_ref, a2_ref,
                      r_ref, k_ref, v_ref, g_ref, dec_ref, a_ref, shift_ref,
                      prev_ref, *, nb, tt):
    rows = nb * tt

    @pl.when(pl.program_id(0) == 0)
    def _():
        prev_ref[...] = xprev_ref[...]

    xn = _rms(x_ref[...], gpre_ref[...])
    if tt > 1:
        shifted = jnp.concatenate([prev_ref[...], xn[:rows - nb]], axis=0)
    else:
        shifted = prev_ref[...]
    prev_ref[...] = xn[rows - nb:]
    shift_ref[...] = xn[rows - nb:]
    xx = shifted - xn
    mix = lambda n: xn + xx * mu_ref[n:n + 1, :]
    r_ref[...] = _mm(mix(0), wr_ref[...])
    k_ref[...] = _mm(mix(2), wk_ref[...])
    v_ref[...] = _mm(mix(3), wv_ref[...])
    g_ref[...] = _silu(_mm(mix(5), wg_ref[...])).astype(g_ref.dtype)
    w_raw = w0_ref[...] + _mm(jnp.tanh(_mm(mix(1), w1_ref[...])), w2_ref[...])
    dec_ref[...] = jnp.exp(-(jax.nn.sigmoid(w_raw) * math.exp(-0.5)))
    a_ref[...] = jax.nn.sigmoid(a0_ref[...] + _mm(_mm(mix(4), a1_ref[...]), a2_ref[...]))


def _rwkv_proj(x, x_prev, gpre, w, *, nb, tt):
    r, d = x.shape
    e = D_INNER
    rows = nb * tt
    lora = w["w1"].shape[1]
    row = lambda i: (i, 0)
    big = pl.BlockSpec((rows, e), row)
    return pl.pallas_call(
        functools.partial(_rwkv_proj_kernel, nb=nb, tt=tt),
        grid=(r // rows,),
        in_specs=[pl.BlockSpec((rows, d), row), _const((nb, d)), _const((1, d)), _const((6, d)),
                  _const((d, e)), _const((d, e)), _const((d, e)), _const((d, e)),
                  _const((1, e)), _const((d, lora)), _const((lora, e)),
                  _const((1, e)), _const((d, lora)), _const((lora, e))],
        out_specs=[big, big, big, big, big, big, _const_out((nb, d))],
        out_shape=[jax.ShapeDtypeStruct((r, e), dt) for dt in (F32, F32, F32, BF16, F32, F32)]
                  + [jax.ShapeDtypeStruct((nb, d), F32)],
        scratch_shapes=[pltpu.VMEM((nb, d), F32)],
        compiler_params=_params("arbitrary"),
        name="rwkv_proj",
    )(x, x_prev, gpre, w["mu"], w["w_r"], w["w_k"], w["w_v"], w["w_g"],
      w["w0"], w["w1"], w["w2"], w["a0"], w["a1"], w["a2"])


RWKV_VSUB = 4
RWKV_VGRP = RWKV_HEAD // RWKV_VSUB
RWKV_BG = 8
RWKV_VBLK = 8
RWKV_UNROLL = 32


def _seg_sum(x):
    lane_axis = x.ndim - 1
    x = x + pltpu.roll(x, 2 * RWKV_HEADS, lane_axis)
    return x + pltpu.roll(x, RWKV_HEADS, lane_axis)


def _rwkv_rec_kernel(r_ref, w_ref, k_ref, v_ref, a_ref, vprev_ref, kk_ref, ka_ref, rk_ref,
                     kkrep_ref, karep_ref, lnw_ref, lnb_ref, s0_ref, o_ref, so_ref,
                     s_ref, vec_ref, sa_scale_ref, bon_ref, yraw_ref, *, tt):
    n = RWKV_HEAD
    nh = RWKV_HEADS
    shape = (RWKV_BG, LANES)
    step = pl.program_id(1)
    last = pl.num_programs(1) - 1

    @pl.when(step == 0)
    def _():
        s_ref[...] = s0_ref[0]
        yraw_ref[...] = jnp.zeros_like(yraw_ref)
        bon_ref[...] = jnp.zeros_like(bon_ref)

    tiles = [slice(j * LANES, (j + 1) * LANES) for j in range(RWKV_VGRP)]

    bon_prev = bon_ref[...]
    tot = yraw_ref[:, :, tiles[0]]
    for sl in tiles[1:]:
        tot = tot + yraw_ref[:, :, sl]
    mean = _seg_sum(tot) * (1.0 / n)
    sq = jnp.zeros((tt,) + shape, F32)
    for sl in tiles:
        yc = yraw_ref[:, :, sl] - mean
        sq = sq + yc * yc
    rstd = lax.rsqrt(_seg_sum(sq) * (1.0 / n) + RWKV_LN_EPS)
    for sl in tiles:
        o_ref[:, :, sl] = ((yraw_ref[:, :, sl] - mean) * rstd * lnw_ref[:, sl] + lnb_ref[:, sl]
                           + bon_prev * vprev_ref[:, :, sl])

    seg = lax.broadcasted_iota(jnp.int32, (tt,) + shape, 2) // nh
    seg_low = seg < 2
    seg_odd = (seg & 1) == 1

    def rep4(x):
        rolled = [x] + [pltpu.roll(x, q * nh, 2) for q in range(1, RWKV_VSUB)]
        pair = [jnp.where(seg_low, rolled[i], rolled[(i + 2) % RWKV_VSUB]) for i in range(RWKV_VSUB)]
        return [jnp.where(seg_odd, pair[(1 - s_) % RWKV_VSUB], pair[(-s_) % RWKV_VSUB])
                for s_ in range(RWKV_VSUB)]

    n2 = jnp.zeros((tt,) + shape, F32)
    bon = jnp.zeros((tt,) + shape, F32)
    for j, sl in enumerate(tiles):
        r = r_ref[:, :, sl]
        k = k_ref[:, :, sl]
        a = a_ref[:, :, sl]
        kk = k * kk_ref[:, sl]
        n2 = n2 + kk * kk
        bon = bon + r * (k * (1.0 + (a - 1.0) * ka_ref[:, sl])) * rk_ref[:, sl]
        r_rep = rep4(r)
        w_rep = rep4(w_ref[:, :, sl])
        k_rep = rep4(k)
        a_rep = rep4(a)
        for s_ in range(RWKV_VSUB):
            c = RWKV_VSUB * j + s_
            kk_c = k_rep[s_] * kkrep_ref[c]
            vec_ref[:, 0, c] = r_rep[s_]
            vec_ref[:, 1, c] = w_rep[s_]
            vec_ref[:, 2, c] = kk_c
            vec_ref[:, 3, c] = kk_c * a_rep[s_]
            vec_ref[:, 4, c] = k_rep[s_] * (1.0 + (a_rep[s_] - 1.0) * karep_ref[c])
    inv = lax.rsqrt(jnp.maximum(_seg_sum(n2), 1e-24))
    sa_scale_ref[...] = -(inv * inv)
    bon_ref[...] = _seg_sum(bon)

    def token(t, carry):
        sa_scale = sa_scale_ref[t]
        zero = jnp.zeros((RWKV_VBLK,) + shape, F32)
        for vb in range(RWKV_VGRP // RWKV_VBLK):
            vsl = slice(vb * RWKV_VBLK, (vb + 1) * RWKV_VBLK)
            vv = jnp.stack([v_ref[t, :, sl] for sl in tiles[vsl]])

            def sa_step(c, acc):
                return acc + s_ref[c, vsl] * vec_ref[t, 2, c]

            sa = lax.fori_loop(0, n, sa_step, zero, unroll=RWKV_UNROLL) * sa_scale

            def update_step(c, acc):
                s_new = s_ref[c, vsl] * vec_ref[t, 1, c] + sa * vec_ref[t, 3, c] + vv * vec_ref[t, 4, c]
                s_ref[c, vsl] = s_new
                return acc + s_new * vec_ref[t, 0, c]

            y = lax.fori_loop(0, n, update_step, zero, unroll=RWKV_UNROLL)
            for i, sl in enumerate(tiles[vsl]):
                yraw_ref[t, :, sl] = y[i]
        return carry

    @pl.when(step < last)
    def _():
        lax.fori_loop(0, tt, token, 0)

    @pl.when(step == last)
    def _():
        so_ref[0] = s_ref[...]


def _rwkv_rec(r, w, k, v, a, pk, s0, *, tt):
    t_len, nb, e = r.shape
    n = RWKV_HEAD
    n_tiles = t_len // tt
    cur = pl.BlockSpec((tt, RWKV_BG, e), lambda bi, si: (jnp.minimum(si, n_tiles - 1), bi, 0))
    prev = pl.BlockSpec((tt, RWKV_BG, e), lambda bi, si: (jnp.maximum(si - 1, 0), bi, 0))
    st = pl.BlockSpec((1, n, RWKV_VGRP, RWKV_BG, LANES), lambda bi, si: (bi, 0, 0, 0, 0))
    return pl.pallas_call(
        functools.partial(_rwkv_rec_kernel, tt=tt),
        grid=(nb // RWKV_BG, n_tiles + 1),
        in_specs=[cur] * 5 + [prev] + [_const((RWKV_BG, e))] * 3
                 + [_const((n, RWKV_BG, LANES))] * 2 + [_const((RWKV_BG, e))] * 2 + [st],
        out_specs=[prev, st],
        out_shape=[jax.ShapeDtypeStruct((t_len, nb, e), F32), jax.ShapeDtypeStruct(s0.shape, F32)],
        scratch_shapes=[pltpu.VMEM((n, RWKV_VGRP, RWKV_BG, LANES), F32),
                        pltpu.VMEM((tt, 5, n, RWKV_BG, LANES), F32),
                        pltpu.VMEM((tt, RWKV_BG, LANES), F32),
                        pltpu.VMEM((tt, RWKV_BG, LANES), F32),
                        pltpu.VMEM((tt, RWKV_BG, e), F32)],
        compiler_params=_params("parallel", "arbitrary"),
        name="rwkv_recurrence",
    )(r, w, k, v, a, v, pk["k_k"], pk["k_a"], pk["r_k"], pk["k_k_rep"], pk["k_a_rep"],
      pk["ln_w"], pk["ln_b"], s0)


def _gated_out_kernel(y_ref, g_ref, x_ref, w_ref, gpost_ref, xo_ref):
    y = y_ref[...] * g_ref[...].astype(F32)
    xo_ref[...] = x_ref[...] + _rms(_mm(y, w_ref[...]), gpost_ref[...])


def _gated_out(y, g, x, w, gpost, *, rows):
    r, d = x.shape
    e = D_INNER
    row = lambda i: (i, 0)
    return pl.pallas_call(
        _gated_out_kernel,
        grid=(r // rows,),
        in_specs=[pl.BlockSpec((rows, e), row), pl.BlockSpec((rows, e), row),
                  pl.BlockSpec((rows, d), row), _const((e, d)), _const((1, d))],
        out_specs=pl.BlockSpec((rows, d), row),
        out_shape=jax.ShapeDtypeStruct((r, d), F32),
        compiler_params=_params("parallel"),
        name="gated_out",
    )(y, g, x, w, gpost)


def _ret_proj_kernel(x_ref, gpre_ref, wq_ref, wk_ref, wv_ref, wg_ref, q_ref, k_ref, v_ref, g_ref):
    xn = _rms(x_ref[...], gpre_ref[...]).astype(BF16)
    dt = q_ref.dtype
    q_ref[...] = jnp.dot(xn, wq_ref[...], preferred_element_type=F32).astype(dt)
    k_ref[...] = jnp.dot(xn, wk_ref[...], preferred_element_type=F32).astype(dt)
    v_ref[...] = jnp.dot(xn, wv_ref[...], preferred_element_type=F32).astype(dt)
    g_ref[...] = _silu(jnp.dot(xn, wg_ref[...], preferred_element_type=F32)).astype(dt)


def _ret_proj(x, gpre, w, *, rows, out_dtype):
    r, d = x.shape
    e = D_INNER
    row = lambda i: (i, 0)
    return pl.pallas_call(
        _ret_proj_kernel,
        grid=(r // rows,),
        in_specs=[pl.BlockSpec((rows, d), row), _const((1, d)), _const((d, d)), _const((d, d)),
                  _const((d, e)), _const((d, e))],
        out_specs=[pl.BlockSpec((rows, d), row), pl.BlockSpec((rows, d), row),
                   pl.BlockSpec((rows, e), row), pl.BlockSpec((rows, e), row)],
        out_shape=[jax.ShapeDtypeStruct((r, d), out_dtype), jax.ShapeDtypeStruct((r, d), out_dtype),
                   jax.ShapeDtypeStruct((r, e), out_dtype), jax.ShapeDtypeStruct((r, e), out_dtype)],
        compiler_params=_params("parallel"),
        name="ret_proj",
    )(x, gpre, w["w_q"], w["w_k"], w["w_v"], w["w_g"])


def _ret_core_kernel(q_ref, kt_ref, v_ref, cos_ref, sin_ref, cost_ref, sint_ref, mask_ref,
                     qdec_ref, kdec_ref, gl_ref, s0_ref, *rest, fuse_out):
    half = RET_DK // 2
    if fuse_out:
        g_ref, x_ref, wo_ref, gpost_ref, out_ref, so_ref, s_ref = rest
    else:
        out_ref, so_ref, s_ref = rest

    @pl.when(pl.program_id(1) == 0)
    def _():
        s_ref[...] = s0_ref[0]

    cos = cos_ref[...]
    sin = sin_ref[...]
    cost = cost_ref[...]
    sint = sint_ref[...]
    proj = None
    for h in range(RET_HEADS):
        q1 = q_ref[:, h * RET_DK:h * RET_DK + half].astype(F32)
        q2 = q_ref[:, h * RET_DK + half:(h + 1) * RET_DK].astype(F32)
        qr = jnp.concatenate([q1 * cos - q2 * sin, q2 * cos + q1 * sin], axis=1)
        k1 = kt_ref[0, h * RET_DK:h * RET_DK + half, :].astype(F32)
        k2 = kt_ref[0, h * RET_DK + half:(h + 1) * RET_DK, :].astype(F32)
        krt = jnp.concatenate([k1 * cost - k2 * sint, k2 * cost + k1 * sint], axis=0)
        krt = krt * (RET_DK ** -0.5)
        vh = v_ref[:, h * RET_DV:(h + 1) * RET_DV].astype(BF16)
        qb = qr.astype(BF16)
        scores = jnp.dot(qb, krt.astype(BF16), preferred_element_type=F32) * mask_ref[h]
        s_old = s_ref[h]
        y = jnp.dot(scores.astype(BF16), vh, preferred_element_type=F32)
        y = y + jnp.dot(qb, s_old.astype(BF16), preferred_element_type=F32) * qdec_ref[h]
        kw = (krt * kdec_ref[h]).astype(BF16)
        s_ref[h] = gl_ref[h] * s_old + jnp.dot(kw, vh, preferred_element_type=F32)
        y = y * lax.rsqrt(jnp.mean(y * y, axis=-1, keepdims=True) + NORM_EPS)
        if fuse_out:
            yg = y * g_ref[:, h * RET_DV:(h + 1) * RET_DV].astype(F32)
            part = _mm(yg, wo_ref[h * RET_DV:(h + 1) * RET_DV, :])
            proj = part if proj is None else proj + part
        else:
            out_ref[:, h * RET_DV:(h + 1) * RET_DV] = y
    if fuse_out:
        out_ref[...] = x_ref[...] + _rms(proj, gpost_ref[...])

    @pl.when(pl.program_id(1) == pl.num_programs(1) - 1)
    def _():
        so_ref[0] = s_ref[...]


def _ret_core(q, kt, v, tabs, s0, *, lc, out=None, drop_first=False):
    b, dkh, t_len = kt.shape
    nc = t_len // lc
    e = D_INNER
    d = D_MODEL
    rowc = lambda bi, ci: (bi * nc + ci, 0)
    state = pl.BlockSpec((1, RET_HEADS, RET_DK, RET_DV), lambda bi, ci: (bi, 0, 0, 0))
    half = RET_DK // 2
    in_specs = [pl.BlockSpec((lc, dkh), rowc),
                pl.BlockSpec((1, dkh, lc), lambda bi, ci: (bi, 0, ci)),
                pl.BlockSpec((lc, e), rowc),
                pl.BlockSpec((lc, half), lambda bi, ci: (ci, 0)),
                pl.BlockSpec((lc, half), lambda bi, ci: (ci, 0)),
                pl.BlockSpec((half, lc), lambda bi, ci: (0, ci)),
                pl.BlockSpec((half, lc), lambda bi, ci: (0, ci)),
                _const((RET_HEADS, lc, lc)), _const((RET_HEADS, lc, 1)),
                _const((RET_HEADS, 1, lc)), _const((RET_HEADS, 1, 1)), state]
    args = [q, kt, v, tabs["cos"], tabs["sin"], tabs["cos_t"], tabs["sin_t"], tabs["mask"],
            tabs["qdec"], tabs["kdec"], tabs["gl"], s0]
    width = e
    if out is not None:
        in_specs += [pl.BlockSpec((lc, e), rowc), pl.BlockSpec((lc, d), rowc), _const((e, d)),
                     _const((1, d))]
        args += list(out)
        width = d
    out_rows, out_map = b * t_len, rowc
    if drop_first:
        out_rows = b * (t_len - lc)
        out_map = lambda bi, ci: (bi * (nc - 1) + jnp.maximum(ci - 1, 0), 0)
    return pl.pallas_call(
        functools.partial(_ret_core_kernel, fuse_out=out is not None),
        grid=(b, nc),
        in_specs=in_specs,
        out_specs=[pl.BlockSpec((lc, width), out_map), state],
        out_shape=[jax.ShapeDtypeStruct((out_rows, width), F32),
                   jax.ShapeDtypeStruct((b, RET_HEADS, RET_DK, RET_DV), F32)],
        scratch_shapes=[pltpu.VMEM((RET_HEADS, RET_DK, RET_DV), F32)],
        compiler_params=_params("parallel", "arbitrary"),
        name="ret_core",
    )(*args)


def _ret_tables(pos, lc):
    half = RET_DK // 2
    inv = ROPE_BASE ** (-jnp.arange(half, dtype=F32) / half)
    ang = pos.astype(F32)[:, None] * inv[None, :]
    cos = jnp.cos(ang)
    sin = jnp.sin(ang)
    log_g = jnp.log1p(-jnp.exp2(-5.0 - jnp.arange(RET_HEADS, dtype=F32)))
    n = jnp.arange(lc, dtype=F32)
    diff = n[:, None] - n[None, :]
    mask = jnp.where(diff[None] >= 0, jnp.exp(diff[None] * log_g[:, None, None]), 0.0)
    qdec = jnp.exp((n[None, :] + 1.0) * log_g[:, None])[:, :, None]
    kdec = jnp.exp((lc - 1.0 - n)[None, :] * log_g[:, None])[:, None, :]
    gl = jnp.exp(lc * log_g)[:, None, None]
    return dict(cos=cos, sin=sin, cos_t=cos.T, sin_t=sin.T, mask=mask, qdec=qdec, kdec=kdec, gl=gl)


def _s5_block_weights(bb_re, bb_im, c_re, c_im):
    gl = S5_GROUPS // S5_KB
    eye = jnp.eye(gl, dtype=BF16)

    def bdiag_b(bb):
        x = bb.astype(BF16).reshape(S5_KB, gl, S5_GROUP, S5_STATE)
        x = x[:, :, :, None, :] * eye[None, :, None, :, None]
        return x.reshape(S5_KB, gl * S5_GROUP, gl * S5_STATE)

    def bdiag_c(c):
        x = jnp.swapaxes(c, 1, 2).astype(BF16).reshape(S5_KB, gl, S5_STATE, S5_GROUP)
        x = x[:, :, :, None, :] * eye[None, :, None, :, None]
        return x.reshape(S5_KB, gl * S5_STATE, gl * S5_GROUP)

    bw = jnp.concatenate([bdiag_b(bb_re), bdiag_b(bb_im)], axis=2)
    cw = jnp.concatenate([bdiag_c(c_re), -bdiag_c(c_im)], axis=1)
    return bw, cw


def _head_minor(x):
    lead = x.shape[:-1]
    x = x.reshape(lead + (RWKV_HEADS, RWKV_HEAD))
    return jnp.swapaxes(x, -1, -2).reshape(lead + (D_INNER,))


def _key_param(p):
    return jnp.broadcast_to(p.reshape(1, D_INNER), (RWKV_BG, D_INNER))


def _key_param_rep(p):
    x = jnp.tile(p.reshape(RWKV_HEAD, 1, RWKV_HEADS), (1, 1, RWKV_VSUB))
    return jnp.broadcast_to(x, (RWKV_HEAD, RWKV_BG, LANES))


def _wkv_to_blocks(s, nb):
    x = s.reshape(nb // RWKV_BG, RWKV_BG, RWKV_HEADS, RWKV_VGRP, RWKV_VSUB, RWKV_HEAD)
    x = jnp.transpose(x, (0, 5, 3, 1, 4, 2))
    return x.reshape(nb // RWKV_BG, RWKV_HEAD, RWKV_VGRP, RWKV_BG, LANES)


def _wkv_from_blocks(x, nb):
    x = x.reshape(nb // RWKV_BG, RWKV_HEAD, RWKV_VGRP, RWKV_BG, RWKV_VSUB, RWKV_HEADS)
    x = jnp.transpose(x, (0, 3, 5, 2, 4, 1))
    return x.reshape(nb, RWKV_HEADS, RWKV_HEAD, RWKV_HEAD)


def _run_stream(x, states, wts, norm_pre, norm_post, *, nb, t_len, pos0, tiles, n_drop):
    d = D_MODEL
    e = D_INNER
    new = {}
    gp = lambda i: (norm_pre[i].reshape(1, d), norm_post[i].reshape(1, d))

    gpre, gpost = gp(0)
    cs = jnp.transpose(states["lru_conv"], (1, 0, 2)).reshape((LRU_CONV_W - 1) * nb, e)
    x, cso, ho = _lru_layer(x, cs, states["lru_h"], gpre, gpost, wts["lru"], nb=nb, tt=tiles["lru_tt"])
    new["lru_conv"] = jnp.transpose(cso.reshape(LRU_CONV_W - 1, nb, e), (1, 0, 2))
    new["lru_h"] = ho

    gpre, gpost = gp(1)
    w = wts["s5"]
    gn = S5_GROUPS * S5_STATE
    y, sre, sim = _s5_core(x, gpre, w, states["s5_re"].reshape(nb, gn),
                           states["s5_im"].reshape(nb, gn), nb=nb, tt=tiles["s5_tt"])
    x = _s5_out(y, x, gpre, gpost, w, rows=tiles["rows"])
    new["s5_re"] = sre.reshape(nb, S5_GROUPS, S5_STATE)
    new["s5_im"] = sim.reshape(nb, S5_GROUPS, S5_STATE)

    gpre, gpost = gp(2)
    w = wts["rwkv"]
    r, k, v, g, dec, a, shift = _rwkv_proj(x, states["rwkv_shift"], gpre, w, nb=nb, tt=tiles["rwkv_tt"])
    as3 = lambda z: z.reshape(t_len, nb, e)
    pk = dict(k_k=_key_param(w["k_k"]), k_a=_key_param(w["k_a"]), r_k=_key_param(w["r_k"]),
              k_k_rep=_key_param_rep(w["k_k"]), k_a_rep=_key_param_rep(w["k_a"]),
              ln_w=_key_param(w["ln_w"]), ln_b=_key_param(w["ln_b"]))
    o, s_new = _rwkv_rec(as3(r), as3(dec), as3(k), as3(v), as3(a), pk,
                         _wkv_to_blocks(states["rwkv_wkv"], nb), tt=tiles["rec_tt"])
    x = _gated_out(o.reshape(t_len * nb, e), g, x, w["w_o"], gpost, rows=tiles["rows"])
    new["rwkv_shift"] = shift
    new["rwkv_wkv"] = _wkv_from_blocks(s_new, nb)

    gpre, gpost = gp(3)
    w = wts["ret"]
    lc = tiles["ret_lc"]
    t_pad = -(-t_len // lc) * lc
    pad = t_pad - t_len
    xb = jnp.transpose(x.reshape(t_len, nb, d), (1, 0, 2))
    xb = jnp.pad(xb, ((0, 0), (pad, 0), (0, 0))).reshape(nb * t_pad, d)
    proj_dtype = BF16 if lc % 16 == 0 else F32
    q, kk, v, g = _ret_proj(xb, gpre, w, rows=tiles["ret_rows"], out_dtype=proj_dtype)
    kt = jnp.transpose(kk.reshape(nb, t_pad, RET_HEADS * RET_DK), (0, 2, 1))
    tabs = _ret_tables(pos0 - pad + jnp.arange(t_pad, dtype=jnp.int32), lc)
    if lc % LANES == 0:
        drop_first = pad + n_drop == lc
        xb, s_new = _ret_core(q, kt, v, tabs, states["ret"], lc=lc, out=(g, xb, w["w_o"], gpost),
                              drop_first=drop_first)
        xb = xb.reshape(nb, -1, d)
        if not drop_first:
            xb = xb[:, pad + n_drop:]
    else:
        y, s_new = _ret_core(q, kt, v, tabs, states["ret"], lc=lc)
        xb = _gated_out(y, g, xb, w["w_o"], gpost, rows=tiles["ret_rows"])
        xb = xb.reshape(nb, t_pad, d)[:, pad + n_drop:]
    new["ret"] = s_new
    return xb, new


def kernel(x_prompt, x_sample, state_lru_conv, state_lru_h, state_s5_re, state_s5_im, state_rwkv_shift, state_rwkv_wkv, state_ret, meta_tokens, norm_pre, norm_post, lru_w_in, lru_conv_w, lru_conv_b, lru_wa, lru_ba, lru_wx, lru_bx, lru_lam, lru_w_out, s5_w_in, s5_log_dt, s5_a_re, s5_a_im, s5_b_re, s5_b_im, s5_c_re, s5_c_im, s5_d, s5_glu_w, s5_glu_b, s5_w_out, rwkv_mu, rwkv_w_r, rwkv_w_k, rwkv_w_v, rwkv_w_g, rwkv_w0, rwkv_w1, rwkv_w2, rwkv_a0, rwkv_a1, rwkv_a2, rwkv_k_k, rwkv_k_a, rwkv_r_k, rwkv_ln_w, rwkv_ln_b, rwkv_w_o, ret_w_q, ret_w_k, ret_w_v, ret_w_g, ret_w_o):
    d = D_MODEL
    e = D_INNER
    bp, seq, _ = x_prompt.shape
    bs, dec_seq, _ = x_sample.shape
    t_p = seq + N_META
    bf = lambda x: x.astype(BF16)
    vec = lambda x: x.reshape(1, -1)
    hm = _head_minor

    ab_re, ab_im, bb_re, bb_im = _s5_discretize(
        s5_log_dt[0], s5_a_re[0], s5_a_im[0],
        jnp.swapaxes(s5_b_re[0], 1, 2), jnp.swapaxes(s5_b_im[0], 1, 2))
    bw, cw = _s5_block_weights(bb_re, bb_im, s5_c_re[0], s5_c_im[0])
    wts = dict(
        lru=dict(w_in=bf(lru_w_in[0]), conv_w=lru_conv_w[0], conv_b=vec(lru_conv_b[0]),
                 wa=bf(lru_wa[0]), ba=vec(lru_ba[0]), wx=bf(lru_wx[0]), bx=vec(lru_bx[0]),
                 lam=vec(lru_lam[0]), w_out=bf(lru_w_out[0])),
        s5=dict(w_u=bf(s5_w_in[0, :, :e]), w_gate=bf(s5_w_in[0, :, e:]), bw=bw, cw=cw, ab_re=vec(ab_re), ab_im=vec(ab_im),
                d=vec(s5_d[0]), glu_w=bf(s5_glu_w[0]), glu_b=vec(s5_glu_b[0]), w_out=bf(s5_w_out[0])),
        rwkv=dict(mu=rwkv_mu[0], w_r=bf(hm(rwkv_w_r[0])), w_k=bf(hm(rwkv_w_k[0])),
                  w_v=bf(hm(rwkv_w_v[0])), w_g=bf(hm(rwkv_w_g[0])), w0=vec(hm(rwkv_w0[0])),
                  w1=bf(rwkv_w1[0]), w2=bf(hm(rwkv_w2[0])), a0=vec(hm(rwkv_a0[0])),
                  a1=bf(rwkv_a1[0]), a2=bf(hm(rwkv_a2[0])),
                  k_k=hm(rwkv_k_k[0]), k_a=hm(rwkv_k_a[0]), r_k=hm(rwkv_r_k[0]),
                  ln_w=hm(rwkv_ln_w[0]), ln_b=hm(rwkv_ln_b[0]), w_o=bf(hm(rwkv_w_o[0].T).T)),
        ret=dict(w_q=bf(ret_w_q[0]), w_k=bf(ret_w_k[0]), w_v=bf(ret_w_v[0]), w_g=bf(ret_w_g[0]),
                 w_o=bf(ret_w_o[0])),
    )

    meta = jnp.broadcast_to(meta_tokens[None].astype(x_prompt.dtype), (bp, N_META, d))
    hp = jnp.concatenate([meta, x_prompt], axis=1)
    z = lambda *s: jnp.zeros(s, F32)
    p_states = dict(lru_conv=z(bp, LRU_CONV_W - 1, e), lru_h=z(bp, e),
                    s5_re=z(bp, S5_GROUPS, S5_STATE), s5_im=z(bp, S5_GROUPS, S5_STATE),
                    rwkv_shift=z(bp, d), rwkv_wkv=z(bp, RWKV_HEADS, RWKV_HEAD, RWKV_HEAD),
                    ret=z(bp, RET_HEADS, RET_DK, RET_DV))
    p_tiles = dict(lru_tt=_divisor_tile(t_p, 48, 1), s5_tt=_divisor_tile(t_p, 48, 1),
                   rwkv_tt=_divisor_tile(t_p, 24, 1), rec_tt=_divisor_tile(t_p, 16, 1),
                   rows=_divisor_tile(t_p * bp, 384, 8), ret_lc=RET_CHUNK,
                   ret_rows=_divisor_tile(bp * (-(-t_p // RET_CHUNK) * RET_CHUNK), 512, 8))
    yp, new_p = _run_stream(hp, p_states, wts, norm_pre, norm_post, nb=bp, t_len=t_p, pos0=0,
                            tiles=p_tiles, n_drop=N_META)

    s_states = dict(lru_conv=state_lru_conv[0], lru_h=state_lru_h[0], s5_re=state_s5_re[0],
                    s5_im=state_s5_im[0], rwkv_shift=state_rwkv_shift[0],
                    rwkv_wkv=state_rwkv_wkv[0], ret=state_ret[0])
    s_rows = _divisor_tile(dec_seq * bs, 256, 8)
    s_tiles = dict(lru_tt=2, s5_tt=1, rwkv_tt=2, rec_tt=dec_seq, rows=s_rows, ret_lc=dec_seq,
                   ret_rows=s_rows)
    ys, new_s = _run_stream(x_sample, s_states, wts, norm_pre, norm_post, nb=bs, t_len=dec_seq,
                            pos0=PAST_LEN, tiles=s_tiles, n_drop=0)

    names = ("lru_conv", "lru_h", "s5_re", "s5_im", "rwkv_shift", "rwkv_wkv", "ret")
    return ((yp, ys)
            + tuple(new_p[n][None] for n in names)
            + tuple(new_s[n][None] for n in names))
```
